```python
import math
import jax, jax.numpy as jnp
from jax import lax
import numpy as np

D_MODEL = 2048
BATCH = 4
SEQ = 2048
DEPTH = 4

GRID_W = 64
CTX_LEN = 256
N_GROUPS = 4
HEADS_PER_GROUP = 4
HEAD_W = D_MODEL // (N_GROUPS * HEADS_PER_GROUP)
GROUP_W = HEADS_PER_GROUP * HEAD_W
D_MIX = N_GROUPS * GROUP_W
A_DK = HEAD_W
A_DV = HEAD_W
B_DH = HEAD_W // 2
C_DH = HEAD_W
NA_KH = 8
NA_KW = 16
D_DK = HEAD_W
D_DV = HEAD_W
D_FF = 5632
CONV_W = 3
CHUNK = 64
Q_BLOCK = 128
ROPE_THETA = 10000.0
EPS = 1e-6
SPLIT_A = (GROUP_W,) * 5
SPLIT_B = (GROUP_W,) * 3
SPLIT_C = (GROUP_W,) * 3
SPLIT_D = (GROUP_W,) * 4 + (4 * HEADS_PER_GROUP,)
GROUP_SPLIT = (sum(SPLIT_A), sum(SPLIT_B), sum(SPLIT_C), sum(SPLIT_D))
D_IN = sum(GROUP_SPLIT)

kernel_name = 'hybrid_flow_backbone'

F32 = jnp.float32


def _split(t, sizes):
    return jnp.split(t, [int(s) for s in np.cumsum(sizes)[:-1]], axis=-1)


def rmsnorm(t, w=None):
    tf = t.astype(F32)
    y = tf * lax.rsqrt(jnp.mean(tf * tf, axis=-1, keepdims=True) + EPS)
    if w is not None:
        y = y * w.astype(F32)
    return y.astype(t.dtype)


def modulate(t, shift, scale):
    return rmsnorm(t) * (1 + scale) + shift


def heads(t):
    b, n, _ = t.shape
    return t.reshape(b, n, HEADS_PER_GROUP, -1).transpose(0, 2, 1, 3)


def merge(t):
    b, h, n, d = t.shape
    return t.transpose(0, 2, 1, 3).reshape(b, n, h * d)


def head_norm(o, w):
    h, d = o.shape[1], o.shape[3]
    return merge(rmsnorm(o.astype(F32)) * w.astype(F32).reshape(h, 1, d))


def flip_seq(t):
    return jnp.flip(t, axis=2)


def to_chunks(t):
    b, h, n = t.shape[:3]
    return jnp.moveaxis(t.reshape(b, h, n // CHUNK, CHUNK, *t.shape[3:]), 2, 0)


def from_chunks(t):
    nc, b, h, L = t.shape[:4]
    return jnp.moveaxis(t, 0, 2).reshape(b, h, nc * L, *t.shape[4:])


def lambda_init(layer_idx):
    return 0.8 - 0.6 * math.exp(-0.3 * layer_idx)


def axial_rope(n):
    t = jnp.arange(n)
    row = (t // GRID_W).astype(F32)
    col = (t % GRID_W).astype(F32)
    half = B_DH // 2
    inv = ROPE_THETA ** (-jnp.arange(0, half, 2, dtype=F32) / half)
    ar = row[:, None] * inv
    ac = col[:, None] * inv
    ang = jnp.concatenate([ar, ar, ac, ac], axis=-1)
    return jnp.cos(ang), jnp.sin(ang)


def apply_rope(t, cos, sin):
    r1, r2, c1, c2 = jnp.split(t, 4, axis=-1)
    rot = jnp.concatenate([-r2, r1, -c2, c1], axis=-1)
    return t * cos.astype(t.dtype) + rot * sin.astype(t.dtype)


def gla_scan(q, k, v, log_f, s0, with_out):
    tri = jnp.tril(jnp.ones((CHUNK, CHUNK), bool))

    def body(S, inp):
        qc, kc, vc, gc = inp
        b = jnp.cumsum(gc, axis=2)
        b_last = b[:, :, -1:]
        S_new = jnp.exp(b_last)[:, :, 0, :, None] * S + jnp.einsum('bhsd,bhse->bhde', kc * jnp.exp(b_last - b), vc)
        if not with_out:
            return S_new, None
        rel = jnp.where(tri[:, :, None], b[:, :, :, None, :] - b[:, :, None, :, :], -jnp.inf)
        A = jnp.einsum('bhtd,bhsd,bhtsd->bhts', qc, kc, jnp.exp(rel))
        o = jnp.einsum('bhtd,bhde->bhte', qc * jnp.exp(b), S) + jnp.einsum('bhts,bhse->bhte', A, vc)
        return S_new, o

    S, o = lax.scan(body, s0, tuple(to_chunks(t) for t in (q, k, v, log_f)))
    return (from_chunks(o) if with_out else None), S


def mlstm_scan(q, k, v, log_i, log_f, state, with_out):
    tri = jnp.tril(jnp.ones((CHUNK, CHUNK), bool))

    def body(carry, inp):
        C, n, m = carry
        qc, kc, vc, ic, fc = inp
        b = jnp.cumsum(fc, axis=-1)
        b_last = b[..., -1]
        upd = b_last[..., None] - b + ic
        m_new = jnp.maximum(b_last + m, jnp.max(upd, axis=-1))
        w_s = jnp.exp(b_last + m - m_new)
        w_u = jnp.exp(upd - m_new[..., None])
        C_new = w_s[..., None, None] * C + jnp.einsum('bhs,bhsd,bhse->bhde', w_u, kc, vc)
        n_new = w_s[..., None] * n + jnp.einsum('bhs,bhsd->bhd', w_u, kc)
        if not with_out:
            return (C_new, n_new, m_new), None
        log_w = jnp.where(tri, b[..., :, None] - b[..., None, :] + ic[..., None, :], -jnp.inf)
        inter = b + m[..., None]
        m_t = jnp.maximum(inter, jnp.max(log_w, axis=-1))
        w_inter = jnp.exp(inter - m_t)
        w_intra = jnp.exp(log_w - m_t[..., None]) * jnp.einsum('bhtd,bhsd->bhts', qc, kc)
        num = w_inter[..., None] * jnp.einsum('bhtd,bhde->bhte', qc, C) + jnp.einsum('bhts,bhse->bhte', w_intra, vc)
        den = w_inter * jnp.einsum('bhtd,bhd->bht', qc, n) + jnp.sum(w_intra, axis=-1)
        h = num / jnp.maximum(jnp.abs(den), jnp.exp(-m_t))[..., None]
        return (C_new, n_new, m_new), h

    state, h = lax.scan(body, state, tuple(to_chunks(t) for t in (q, k, v, log_i, log_f)))
    return (from_chunks(h) if with_out else None), state


def hgrn2_mixer(p_x, p_c, lb, norm_w, ctx_out):
    log_lb, log_ub = jnp.log(lb), jnp.log1p(-lb)

    def prep(p):
        q, i, zf, zb, g = _split(p.astype(F32), SPLIT_A)
        q = heads(jax.nn.silu(q)) * A_DK ** -0.5
        lf = [heads(jnp.logaddexp(log_lb[d], log_ub[d] + jax.nn.log_sigmoid(z))) for d, z in enumerate((zf, zb))]
        kk = [-jnp.expm1(t) for t in lf]
        return q, heads(i), kk, lf, g

    qx, vx, kx, lfx, gx = prep(p_x)
    qc, vc, kc, lfc, gc = prep(p_c)
    s0 = jnp.zeros((qc.shape[0], HEADS_PER_GROUP, A_DK, A_DV), F32)
    oc_f, sc_f = gla_scan(qc, kc[0], vc, lfc[0], s0, ctx_out)
    oc_b, sc_b = gla_scan(flip_seq(qc), flip_seq(kc[1]), flip_seq(vc), flip_seq(lfc[1]), s0, ctx_out)
    ox_f, _ = gla_scan(qx, kx[0], vx, lfx[0], sc_f, True)
    ox_b, _ = gla_scan(flip_seq(qx), flip_seq(kx[1]), flip_seq(vx), flip_seq(lfx[1]), sc_b, True)
    y_x = (head_norm(ox_f + flip_seq(ox_b), norm_w) * jax.nn.silu(gx)).astype(p_x.dtype)
    y_c = (head_norm(oc_f + flip_seq(oc_b), norm_w) * jax.nn.silu(gc)).astype(p_c.dtype) if ctx_out else None
    return y_x, y_c


def diff_attention(p_x, p_c, lam, norm_w, lam_init, cos, sin, ctx_out):
    def prep(p, rope):
        q, k, v = _split(p, SPLIT_B)
        b, n, _ = q.shape
        q = q.reshape(b, n, HEADS_PER_GROUP, 2, B_DH).transpose(0, 2, 3, 1, 4)
        k = k.reshape(b, n, HEADS_PER_GROUP, 2, B_DH).transpose(0, 2, 3, 1, 4)
        if rope:
            q, k = apply_rope(q, cos, sin), apply_rope(k, cos, sin)
        return q, k, heads(v)

    qx, kx, vx = prep(p_x, True)
    qc, kc, vc = prep(p_c, False)
    lf = lam.astype(F32)
    lam_full = jnp.exp(jnp.sum(lf[0] * lf[1])) - jnp.exp(jnp.sum(lf[2] * lf[3])) + lam_init
    scale = B_DH ** -0.5

    def attend(q, k, v):
        s = jnp.einsum('bhcqd,bhckd->bhcqk', q, k).astype(F32) * scale
        p = jax.nn.softmax(s, axis=-1)
        w = p[:, :, 0] - lam_full * p[:, :, 1]
        return jnp.einsum('bhqk,bhke->bhqe', w.astype(v.dtype), v)

    k_all = jnp.concatenate([kx, kc], axis=3)
    v_all = jnp.concatenate([vx, vc], axis=2)
    b, h, _, n, dh = qx.shape
    nb = n // Q_BLOCK
    qb = qx.reshape(b, h, 2, nb, Q_BLOCK, dh).transpose(3, 0, 1, 2, 4, 5)
    ob = lax.map(lambda qq: attend(qq, k_all, v_all), qb)
    o_x = ob.transpose(1, 2, 0, 3, 4).reshape(b, h, n, 2 * B_DH)
    y_x = (head_norm(o_x, norm_w) * (1 - lam_init)).astype(p_x.dtype)
    y_c = (head_norm(attend(qc, kc, vc), norm_w) * (1 - lam_init)).astype(p_c.dtype) if ctx_out else None
    return y_x, y_c


def neighbourhood_attention(p_x, p_c, rpb, rows, ctx_out):
    qx, kx, vx = [heads(t) for t in _split(p_x, SPLIT_C)]
    qc, kc, vc = [heads(t) for t in _split(p_c, SPLIT_C)]
    b, h, n, dh = qx.shape
    kh = min(NA_KH, rows)
    scale = C_DH ** -0.5
    qg = qx.reshape(b, h, rows, GRID_W, dh) * scale
    kg = kx.reshape(b, h, rows, GRID_W, dh)
    vg = vx.reshape(b, h, rows, GRID_W, dh)
    r = jnp.arange(rows)
    w = jnp.arange(GRID_W)
    row_idx = jnp.clip(r - kh // 2, 0, rows - kh)[:, None] + jnp.arange(kh)[None, :]
    col_start = jnp.clip(w - NA_KW // 2, 0, GRID_W - NA_KW)
    col_ok = (w[None, :] >= col_start[:, None]) & (w[None, :] < col_start[:, None] + NA_KW)
    k_band = kg[:, :, row_idx]
    v_band = vg[:, :, row_idx]
    s_band = jnp.einsum('bhrqd,bhrkwd->bhrqkw', qg, k_band).astype(F32)
    roff = row_idx - r[:, None] + (NA_KH - 1)
    coff = jnp.clip(w[None, :] - w[:, None], -(NA_KW - 1), NA_KW - 1) + (NA_KW - 1)
    bias = rpb.astype(F32)[:, roff[:, None, :, None], coff[None, :, None, :]]
    s_band = jnp.where(col_ok[:, None, :], s_band + bias, -jnp.inf)
    s_ctx = jnp.einsum('bhrqd,bhcd->bhrqc', qg, kc).astype(F32)
    s = jnp.concatenate([s_band.reshape(b, h, rows, GRID_W, kh * GRID_W), s_ctx], axis=-1)
    p = jax.nn.softmax(s, axis=-1).astype(vx.dtype)
    p_band = p[..., :kh * GRID_W].reshape(b, h, rows, GRID_W, kh, GRID_W)
    p_ctx = p[..., kh * GRID_W:]
    o = jnp.einsum('bhrqkw,bhrkwd->bhrqd', p_band, v_band) + jnp.einsum('bhrqc,bhcd->bhrqd', p_ctx, vc)
    y_x = merge(o.reshape(b, h, n, dh))
    y_c = None
    if ctx_out:
        pc = jax.nn.softmax(jnp.einsum('bhqd,bhkd->bhqk', qc, kc).astype(F32) * scale, axis=-1).astype(vc.dtype)
        y_c = merge(jnp.einsum('bhqk,bhkd->bhqd', pc, vc))
    return y_x, y_c


def mlstm_mixer(p_x, p_c, f_bias, norm_w, ctx_out):
    fb = f_bias.astype(F32)[:, :, None]

    def prep(p):
        q, k, v, o, gates = _split(p.astype(F32), SPLIT_D)
        b, n, _ = gates.shape
        gates = gates.reshape(b, n, 4, HEADS_PER_GROUP).transpose(2, 0, 3, 1)
        log_i = (gates[0], gates[2])
        log_f = (jax.nn.log_sigmoid(gates[1] + fb[0]), jax.nn.log_sigmoid(gates[3] + fb[1]))
        return heads(q) * D_DK ** -0.5, heads(k), heads(v), heads(jax.nn.sigmoid(o)), log_i, log_f

    qx, kx, vx, ox, ix, fx = prep(p_x)
    qc, kc, vc, oc, ic, fc = prep(p_c)
    b = qc.shape[0]
    st0 = (jnp.zeros((b, HEADS_PER_GROUP, D_DK, D_DV), F32), jnp.zeros((b, HEADS_PER_GROUP, D_DK), F32),
           jnp.zeros((b, HEADS_PER_GROUP), F32))
    hc_f, st_f = mlstm_scan(qc, kc, vc, ic[0], fc[0], st0, ctx_out)
    hc_b, st_b = mlstm_scan(flip_seq(qc), flip_seq(kc), flip_seq(vc), flip_seq(ic[1]), flip_seq(fc[1]), st0, ctx_out)
    hx_f, _ = mlstm_scan(qx, kx, vx, ix[0], fx[0], st_f, True)
    hx_b, _ = mlstm_scan(flip_seq(qx), flip_seq(kx), flip_seq(vx), flip_seq(ix[1]), flip_seq(fx[1]), st_b, True)
    y_x = head_norm(ox * (hx_f + flip_seq(hx_b)), norm_w).astype(p_x.dtype)
    y_c = head_norm(oc * (hc_f + flip_seq(hc_b)), norm_w).astype(p_c.dtype) if ctx_out else None
    return y_x, y_c


def conv_ffn(h, w_up, conv_w, conv_b, w_down):
    u = h @ w_up
    n = u.shape[1]
    pad = CONV_W // 2
    up = jnp.pad(u, ((0, 0), (pad, pad), (0, 0)))
    u = sum(up[:, j:j + n] * conv_w[j] for j in range(CONV_W)) + conv_b
    a, g = jnp.split(u, 2, axis=-1)
    return (jax.nn.silu(a) * g) @ w_down


def setup_inputs(seed: int = 0) -> dict:
    key = jax.random.key(seed)
    ks = jax.random.split(key, 20)

    def nrm(k, shape, s):
        return jax.random.normal(k, shape, F32) * s

    return {
        'x': nrm(ks[0], (BATCH, SEQ, D_MODEL), 1.0),
        'c': nrm(ks[1], (BATCH, D_MODEL), 1.0),
        'ctx': nrm(ks[2], (BATCH, CTX_LEN, D_MODEL), 1.0),
        'c_ctx': nrm(ks[3], (D_MODEL,), 1.0),
        'w_ada': nrm(ks[4], (DEPTH, D_MODEL, 6 * D_MODEL), 0.5 * D_MODEL ** -0.5),
        'b_ada': nrm(ks[5], (DEPTH, 6 * D_MODEL), 0.02),
        'w_in': nrm(ks[6], (DEPTH, D_MODEL, D_IN), D_MODEL ** -0.5),
        'hgrn_lb': nrm(ks[7], (DEPTH, 2, GROUP_W), 0.1),
        'hgrn_norm_w': 1.0 + nrm(ks[8], (DEPTH, GROUP_W), 0.1),
        'diff_lam': nrm(ks[9], (DEPTH, 4, B_DH), 0.1),
        'diff_norm_w': 1.0 + nrm(ks[10], (DEPTH, GROUP_W), 0.1),
        'na_rpb': nrm(ks[11], (DEPTH, HEADS_PER_GROUP, 2 * NA_KH - 1, 2 * NA_KW - 1), 0.02),
        'mlstm_f_bias': 3.0 + nrm(ks[12], (DEPTH, 2, HEADS_PER_GROUP), 0.5),
        'mlstm_norm_w': 1.0 + nrm(ks[13], (DEPTH, GROUP_W), 0.1),
        'w_out': nrm(ks[14], (DEPTH, D_MIX, D_MODEL), D_MIX ** -0.5),
        'w_up': nrm(ks[15], (DEPTH, D_MODEL, 2 * D_FF), D_MODEL ** -0.5),
        'conv_w': nrm(ks[16], (DEPTH, CONV_W, 2 * D_FF), CONV_W ** -0.5),
        'conv_b': nrm(ks[17], (DEPTH, 2 * D_FF), 0.02),
        'w_down': nrm(ks[18], (DEPTH, D_FF, D_MODEL), D_FF ** -0.5),
        'final_norm_w': 1.0 + nrm(ks[19], (D_MODEL,), 0.1),
    }


def reference(x, c, ctx, c_ctx, w_ada, b_ada, w_in, hgrn_lb, hgrn_norm_w, diff_lam, diff_norm_w, na_rpb,
              mlstm_f_bias, mlstm_norm_w, w_out, w_up, conv_w, conv_b, w_down, final_norm_w):
    n = x.shape[1]
    rows = n // GRID_W
    cos, sin = axial_rope(n)
    lb = jnp.cumsum(jax.nn.softmax(hgrn_lb.astype(F32), axis=0), axis=0)
    lb = lb - lb[:1]
    cx = ctx
    for l in range(DEPTH):
        ctx_out = l < DEPTH - 1
        mod_x = (jax.nn.silu(c) @ w_ada[l] + b_ada[l])[:, None, :]
        mod_c = jax.nn.silu(c_ctx) @ w_ada[l] + b_ada[l]
        sh1, sc1, g1, sh2, sc2, g2 = jnp.split(mod_x, 6, axis=-1)
        csh1, csc1, cg1, csh2, csc2, cg2 = jnp.split(mod_c, 6, axis=-1)
        pxa, pxb, pxc, pxd = _split(modulate(x, sh1, sc1) @ w_in[l], GROUP_SPLIT)
        pca, pcb, pcc, pcd = _split(modulate(cx, csh1, csc1) @ w_in[l], GROUP_SPLIT)
        ya_x, ya_c = hgrn2_mixer(pxa, pca, lb[l], hgrn_norm_w[l], ctx_out)
        yb_x, yb_c = diff_attention(pxb, pcb, diff_lam[l], diff_norm_w[l], lambda_init(l), cos, sin, ctx_out)
        yc_x, yc_c = neighbourhood_attention(pxc, pcc, na_rpb[l], rows, ctx_out)
        yd_x, yd_c = mlstm_mixer(pxd, pcd, mlstm_f_bias[l], mlstm_norm_w[l], ctx_out)
        x = x + g1 * (jnp.concatenate([ya_x, yb_x, yc_x, yd_x], axis=-1) @ w_out[l])
        x = x + g2 * conv_ffn(modulate(x, sh2, sc2), w_up[l], conv_w[l], conv_b[l], w_down[l])
        if ctx_out:
            cx = cx + cg1 * (jnp.concatenate([ya_c, yb_c, yc_c, yd_c], axis=-1) @ w_out[l])
            cx = cx + cg2 * conv_ffn(modulate(cx, csh2, csc2), w_up[l], conv_w[l], conv_b[l], w_down[l])
    return rmsnorm(x, final_norm_w)
```

```python
import functools
import math

import numpy as np
import jax
import jax.numpy as jnp
from jax import lax
from jax.experimental import pallas as pl
from jax.experimental.pallas import tpu as pltpu

F32 = jnp.float32
BF16 = jnp.bfloat16

HEADS = 4
HEAD_W = 128
GROUP_W = HEADS * HEAD_W
GRID_W = 64
NA_KH = 8
NA_KW = 16
ROPE_THETA = 10000.0
EPS = 1e-6
CONV_W = 3
D_MAIN = 15 * GROUP_W
N_GATES = 4 * HEADS

VMEM_LIMIT_BYTES = 56 * 1024 * 1024
NEG = -1e30

HGRN_CHUNK = 64
HGRN_SUB = 16
MLSTM_CHUNK = 128
ROW_TILE = 768
FF_TILE = 512
HALO = 16
FFN_SUB = 256
MOD_SUB = 256


def _cparams(*sem):
    return pltpu.CompilerParams(dimension_semantics=sem, vmem_limit_bytes=VMEM_LIMIT_BYTES)


def _dot(a, b):
    return jnp.dot(a, b, preferred_element_type=F32)


def _dot_nt(a, b):
    return lax.dot_general(a, b, (((1,), (1,)), ((), ())), preferred_element_type=F32)


def _dot_tn(a, b):
    return lax.dot_general(a, b, (((0,), (0,)), ((), ())), preferred_element_type=F32)


def _silu(t):
    return t * jax.nn.sigmoid(t)


def _log_sigmoid(z):
    return jnp.minimum(z, 0.0) - jnp.log1p(jnp.exp(-jnp.abs(z)))


def _rms(t):
    return t * lax.rsqrt(jnp.mean(t * t, axis=-1, keepdims=True) + EPS)


def _split3(t):
    hi = t.astype(BF16)
    r1 = t - hi.astype(F32)
    mid = r1.astype(BF16)
    lo = (r1 - mid.astype(F32)).astype(BF16)
    return hi, mid, lo


def _modulate(x, mx_ref, mc_ref, row0, k, n_ctx):
    rows = row0 + lax.broadcasted_iota(jnp.int32, (x.shape[0], 1), 0)
    is_ctx = rows < n_ctx
    shift = jnp.where(is_ctx, mc_ref[k:k + 1, :], mx_ref[k:k + 1, :])
    scale = jnp.where(is_ctx, mc_ref[k + 1:k + 2, :], mx_ref[k + 1:k + 2, :])
    return _rms(x) * (1.0 + scale) + shift


def _modulate_into(dst_ref, dst_off, x_ref, mx_ref, mc_ref, row0, k, n_ctx):
    n = x_ref.shape[0]
    sub = math.gcd(n, MOD_SUB)
    for r in range(0, n, sub):
        h = _modulate(x_ref[r:r + sub, :], mx_ref, mc_ref, row0 + r, k, n_ctx)
        dst_ref[dst_off + r:dst_off + r + sub, :] = h.astype(dst_ref.dtype)


def _ada_kernel(c_ref, w_ref, b_ref, o_ref):
    s = _silu(c_ref[...]).astype(BF16)
    o_ref[...] = _dot(s, w_ref[...].astype(BF16)) + b_ref[...]


def _ada(cc, w_ada, b_ada):
    depth, d, n = w_ada.shape
    tn = 1024
    return pl.pallas_call(
        _ada_kernel,
        grid=(depth, n // tn),
        in_specs=[
            pl.BlockSpec((8, d), lambda l, j: (0, 0)),
            pl.BlockSpec((None, d, tn), lambda l, j: (l, 0, j)),
            pl.BlockSpec((None, 1, tn), lambda l, j: (l, 0, j)),
        ],
        out_specs=pl.BlockSpec((None, 8, tn), lambda l, j: (l, 0, j)),
        out_shape=jax.ShapeDtypeStruct((depth, 8, n), F32),
        compiler_params=_cparams("arbitrary", "arbitrary"),
    )(cc, w_ada, b_ada.reshape(depth, 1, n))


def _inproj_kernel(x_ref, mx_ref, mc_ref, w_ref, wg_ref, p_ref, gate_ref, h_ref, *, n_ctx, tm):
    i = pl.program_id(1)
    j = pl.program_id(2)

    @pl.when(j == 0)
    def _():
        _modulate_into(h_ref, 0, x_ref, mx_ref, mc_ref, i * tm, 0, n_ctx)
        gate_ref[...] = _dot(h_ref[...], wg_ref[...])

    p_ref[...] = _dot(h_ref[...], w_ref[...])


def _inproj(xs, mods, w_in, w_gate, l, n_ctx):
    b, t, d = xs.shape
    tm, tn = ROW_TILE, GROUP_W
    nb = mods.shape[1] - 1
    return pl.pallas_call(
        functools.partial(_inproj_kernel, n_ctx=n_ctx, tm=tm),
        grid=(b, t // tm, D_MAIN // tn),
        in_specs=[
            pl.BlockSpec((None, tm, d), lambda bi, i, j: (bi, i, 0)),
            pl.BlockSpec((None, None, 6, d), lambda bi, i, j: (l, bi, 0, 0)),
            pl.BlockSpec((None, None, 6, d), lambda bi, i, j: (l, nb, 0, 0)),
            pl.BlockSpec((None, d, tn), lambda bi, i, j: (l, 0, j)),
            pl.BlockSpec((None, d, HEAD_W), lambda bi, i, j: (l, 0, 0)),
        ],
        out_specs=[
            pl.BlockSpec((None, tm, tn), lambda bi, i, j: (bi, i, j)),
            pl.BlockSpec((None, tm, HEAD_W), lambda bi, i, j: (bi, i, 0)),
        ],
        out_shape=[
            jax.ShapeDtypeStruct((b, t, D_MAIN), F32),
            jax.ShapeDtypeStruct((b, t, HEAD_W), F32),
        ],
        scratch_shapes=[pltpu.VMEM((tm, d), BF16)],
        compiler_params=_cparams("arbitrary", "arbitrary", "arbitrary"),
    )(xs, mods, mods, w_in, w_gate)


def _hgrn_chunk(q_ref, v_ref, z_ref, gp_ref, st_ref, o_ref, r0, rev):
    L, SC = HGRN_CHUNK, HGRN_SUB
    rows = pl.ds(r0, L)
    q = _silu(q_ref[rows, :]) * (HEAD_W ** -0.5)
    z = z_ref[rows, :]
    v = v_ref[rows, :].astype(BF16)
    a = gp_ref[1:2, :] + _log_sigmoid(z)
    ll = gp_ref[0:1, :]
    lf = jnp.maximum(ll, a) + jnp.log1p(jnp.exp(-jnp.abs(ll - a)))
    k = gp_ref[2:3, :] * jax.nn.sigmoid(-z)

    t_i = lax.broadcasted_iota(jnp.int32, (L, L), 0)
    s_i = lax.broadcasted_iota(jnp.int32, (L, L), 1)
    t_blk = t_i & ~(SC - 1)
    if not rev:
        local = (s_i <= t_i) & (s_i >= t_blk)
        before = s_i < t_blk
        causal = s_i <= t_i
    else:
        local = (s_i >= t_i) & (s_i < t_blk + SC)
        before = s_i >= t_blk + SC
        causal = s_i >= t_i
    sel = jnp.concatenate([jnp.where(local, 1.0, 0.0), jnp.where(before, 1.0, 0.0)], axis=0).astype(BF16)
    hi, mid, lo = _split3(lf)
    sums = _dot(sel, hi) + _dot(sel, mid) + _dot(sel, lo)
    bl = sums[:L]
    ref = sums[L:]
    b = bl + ref

    qt = q * jnp.exp(bl)
    qb = (qt * jnp.exp(ref)).astype(BF16)
    st = st_ref[...]
    o = _dot_nt(qb, st.astype(BF16))

    s_col = lax.broadcasted_iota(jnp.int32, (L, 1), 0)
    parts = []
    for blk in range(L // SC):
        ref_blk = ref[blk * SC:blk * SC + 1, :]
        seen = (s_col < (blk + 1) * SC) if not rev else (s_col >= blk * SC)
        kk = (k * jnp.exp(jnp.where(seen, ref_blk - b, NEG))).astype(BF16)
        parts.append(_dot_nt(qt[blk * SC:(blk + 1) * SC].astype(BF16), kk))
    amat = jnp.where(causal, jnp.concatenate(parts, axis=0), 0.0)
    o = o + _dot(amat.astype(BF16), v)
    o_ref[rows, :] = o

    b_end = b[L - 1:L, :] if not rev else b[0:1, :]
    kd = (k * jnp.exp(b_end - b)).astype(BF16)
    st_ref[...] = st * jnp.exp(b_end) + _dot_tn(v, kd)


def _hgrn_kernel(q_ref, v_ref, zf_ref, zb_ref, g_ref, gp_ref, nw_ref, y_ref,
                 of_ref, ob_ref, sf_ref, sb_ref, *, n_ctx, n_tok):
    L = HGRN_CHUNK
    nc, ncc = n_tok // L, n_ctx // L
    sf_ref[...] = jnp.zeros_like(sf_ref)
    sb_ref[...] = jnp.zeros_like(sb_ref)

    def body(step, carry):
        _hgrn_chunk(q_ref, v_ref, zf_ref, gp_ref.at[0:3], sf_ref, of_ref, pl.multiple_of(step * L, L), False)
        cb = jnp.where(step < ncc, ncc - 1 - step, nc - 1 - (step - ncc))
        _hgrn_chunk(q_ref, v_ref, zb_ref, gp_ref.at[3:6], sb_ref, ob_ref, pl.multiple_of(cb * L, L), True)
        return carry

    lax.fori_loop(0, nc, body, 0)

    fr = 256
    def fin(i, carry):
        rows = pl.ds(pl.multiple_of(i * fr, fr), fr)
        o = of_ref[rows, :] + ob_ref[rows, :]
        y_ref[rows, :] = (_rms(o) * nw_ref[...] * _silu(g_ref[rows, :])).astype(y_ref.dtype)
        return carry

    lax.fori_loop(0, n_tok // fr, fin, 0)


def _hgrn(p, gate_par, norm_w, l, n_ctx):
    b, t, _ = p.shape
    nh = GROUP_W // HEAD_W

    def col(seg):
        return pl.BlockSpec((None, t, HEAD_W), lambda bi, h: (bi, 0, seg * nh + h))

    return pl.pallas_call(
        functools.partial(_hgrn_kernel, n_ctx=n_ctx, n_tok=t),
        grid=(b, HEADS),
        in_specs=[col(0), col(1), col(2), col(3), col(4),
                  pl.BlockSpec((None, 8, HEAD_W), lambda bi, h: (l, 0, h)),
                  pl.BlockSpec((None, 1, HEAD_W), lambda bi, h: (l, 0, h))],
        out_specs=pl.BlockSpec((None, t, HEAD_W), lambda bi, h: (bi, 0, h)),
        out_shape=jax.ShapeDtypeStruct((b, t, GROUP_W), BF16),
        scratch_shapes=[pltpu.VMEM((t, HEAD_W), F32), pltpu.VMEM((t, HEAD_W), F32),
                        pltpu.VMEM((HEAD_W, HEAD_W), F32), pltpu.VMEM((HEAD_W, HEAD_W), F32)],
        compiler_params=_cparams("arbitrary", "arbitrary"),
    )(p, p, p, p, p, gate_par, norm_w)


def _rope(t, cos, sin_signed):
    lane = lax.broadcasted_iota(jnp.int32, t.shape, 1)
    first = (lane & 31) < 16
    rot = jnp.where(first, pltpu.roll(t, HEAD_W - 16, 1), pltpu.roll(t, 16, 1))
    return t * cos + rot * sin_signed


def _softmax_rows(s):
    e = jnp.exp(s - jnp.max(s, axis=-1, keepdims=True))
    return e * (1.0 / jnp.sum(e, axis=-1, keepdims=True))


def _diff_kernel(q_ref, k_ref, v_ref, cos_ref, sin_ref, lam_ref, nw_ref, y_ref, kb_ref, vb_ref,
                 *, n_ctx, tq, lam_init):
    qi = pl.program_id(2)

    @pl.when(qi == 0)
    def _():
        kb_ref[...] = _rope(k_ref[...], cos_ref[...], sin_ref[...]).astype(BF16)
        vb_ref[...] = v_ref[...].astype(BF16)

    lam = lam_ref[...]
    lam_full = (jnp.exp(jnp.sum(lam[0:1] * lam[1:2], axis=-1, keepdims=True))
                - jnp.exp(jnp.sum(lam[2:3] * lam[3:4], axis=-1, keepdims=True)) + lam_init)
    rows = pl.ds(pl.multiple_of(qi * tq, tq), tq)
    q = _rope(q_ref[...], cos_ref[rows, :], sin_ref[rows, :]) * ((HEAD_W // 2) ** -0.5)
    lane = lax.broadcasted_iota(jnp.int32, q.shape, 1)
    q0 = jnp.where(lane < HEAD_W // 2, q, 0.0).astype(BF16)
    q1 = jnp.where(lane >= HEAD_W // 2, q, 0.0).astype(BF16)

    def attend(kb, vb):
        p0 = _softmax_rows(_dot_nt(q0, kb))
        p1 = _softmax_rows(_dot_nt(q1, kb))
        o = _dot((p0 - lam_full * p1).astype(BF16), vb)
        y_ref[...] = (_rms(o) * nw_ref[...] * (1.0 - lam_init)).astype(y_ref.dtype)

    @pl.when(qi == 0)
    def _():
        attend(kb_ref[0:n_ctx, :], vb_ref[0:n_ctx, :])

    @pl.when(qi > 0)
    def _():
        attend(kb_ref[...], vb_ref[...])


def _diff(p, cos, sin_signed, lam, norm_w, l, n_ctx, lam_init):
    b, t, _ = p.shape
    tq = n_ctx
    base = 5 * HEADS

    def col(seg, rows, rmap):
        return pl.BlockSpec((None, rows, HEAD_W), lambda bi, h, qi: (bi, rmap(qi), base + seg * HEADS + h))

    return pl.pallas_call(
        functools.partial(_diff_kernel, n_ctx=n_ctx, tq=tq, lam_init=lam_init),
        grid=(b, HEADS, t // tq),
        in_specs=[col(0, tq, lambda qi: qi), col(1, t, lambda qi: 0), col(2, t, lambda qi: 0),
                  pl.BlockSpec((t, HEAD_W), lambda bi, h, qi: (0, 0)),
                  pl.BlockSpec((t, HEAD_W), lambda bi, h, qi: (0, 0)),
                  pl.BlockSpec((None, 8, HEAD_W), lambda bi, h, qi: (l, 0, 0)),
                  pl.BlockSpec((None, 1, HEAD_W), lambda bi, h, qi: (l, 0, h))],
        out_specs=pl.BlockSpec((None, tq, HEAD_W), lambda bi, h, qi: (bi, qi, h)),
        out_shape=jax.ShapeDtypeStruct((b, t, GROUP_W), BF16),
        scratch_shapes=[pltpu.VMEM((t, HEAD_W), BF16), pltpu.VMEM((t, HEAD_W), BF16)],
        compiler_params=_cparams("arbitrary", "arbitrary", "arbitrary"),
    )(p, p, p, cos, sin_signed, lam, norm_w)


def _na_kernel(q_ref, k_ref, v_ref, bias_ref, y_ref, kb_ref, vb_ref, *, n_ctx, grid_rows, kh):
    scale = HEAD_W ** -0.5
    kb_ref[...] = k_ref[...].astype(BF16)
    vb_ref[...] = v_ref[...].astype(BF16)
    kc = kb_ref[0:n_ctx, :]
    vc = vb_ref[0:n_ctx, :]

    qc = (q_ref[0:n_ctx, :] * scale).astype(BF16)
    pc = _softmax_rows(_dot_nt(qc, kc))
    y_ref[0:n_ctx, :] = _dot(pc.astype(BF16), vc).astype(y_ref.dtype)

    win = kh * GRID_W

    def body(r, carry):
        rows = pl.ds(pl.multiple_of(n_ctx + r * GRID_W, GRID_W), GRID_W)
        q = (q_ref[rows, :] * scale).astype(BF16)
        start = jnp.clip(r - kh // 2, 0, grid_rows - kh)
        wrows = pl.ds(pl.multiple_of(n_ctx + start * GRID_W, GRID_W), win)
        sb = _dot_nt(q, kb_ref[wrows, :]) + bias_ref[r - start]
        sc = _dot_nt(q, kc)
        m = jnp.maximum(jnp.max(sb, axis=-1, keepdims=True), jnp.max(sc, axis=-1, keepdims=True))
        eb = jnp.exp(sb - m)
        ec = jnp.exp(sc - m)
        inv = 1.0 / (jnp.sum(eb, axis=-1, keepdims=True) + jnp.sum(ec, axis=-1, keepdims=True))
        o = _dot((eb * inv).astype(BF16), vb_ref[wrows, :]) + _dot((ec * inv).astype(BF16), vc)
        y_ref[rows, :] = o.astype(y_ref.dtype)
        return carry

    lax.fori_loop(0, grid_rows, body, 0)


def _na(p, bias, l, n_ctx):
    b, t, _ = p.shape
    grid_rows = (t - n_ctx) // GRID_W
    kh = min(NA_KH, grid_rows)
    base = 8 * HEADS

    def col(seg):
        return pl.BlockSpec((None, t, HEAD_W), lambda bi, h: (bi, 0, base + seg * HEADS + h))

    return pl.pallas_call(
        functools.partial(_na_kernel, n_ctx=n_ctx, grid_rows=grid_rows, kh=kh),
        grid=(b, HEADS),
        in_specs=[col(0), col(1), col(2),
                  pl.BlockSpec((None, None, kh, GRID_W, kh * GRID_W), lambda bi, h: (l, h, 0, 0, 0))],
        out_specs=pl.BlockSpec((None, t, HEAD_W), lambda bi, h: (bi, 0, h)),
        out_shape=jax.ShapeDtypeStruct((b, t, GROUP_W), BF16),
        scratch_shapes=[pltpu.VMEM((t, HEAD_W), BF16), pltpu.VMEM((t, HEAD_W), BF16)],
        compiler_params=_cparams("arbitrary", "arbitrary"),
    )(p, p, p, bias)


def _mlstm_chunk(q_ref, k_ref, v_ref, gc_ref, gr_ref, fb, c_ref, m_ref, h_ref, r0, rev):
    L = MLSTM_CHUNK
    rows = pl.ds(r0, L)
    gi, gf = (2, 3) if rev else (0, 1)
    q = (q_ref[rows, :] * (HEAD_W ** -0.5)).astype(BF16)
    k = k_ref[rows, :]
    v = v_ref[rows, :]
    vaug = jnp.concatenate([v, jnp.ones_like(v)], axis=1).astype(BF16)
    gcol = gc_ref[rows, :]
    grow = gr_ref[:, rows]
    i_col = gcol[:, gi:gi + 1]
    f_col = _log_sigmoid(gcol[:, gf:gf + 1] + fb)
    i_row = grow[gi:gi + 1, :]
    f_row = _log_sigmoid(grow[gf:gf + 1, :] + fb)

    t_i = lax.broadcasted_iota(jnp.int32, (L, L), 0)
    s_i = lax.broadcasted_iota(jnp.int32, (L, L), 1)
    causal = (s_i <= t_i) if not rev else (s_i >= t_i)
    causal_t = (t_i <= s_i) if not rev else (t_i >= s_i)
    b_col = jnp.sum(jnp.where(causal, f_row, 0.0), axis=1, keepdims=True)
    b_row = jnp.sum(jnp.where(causal_t, f_col, 0.0), axis=0, keepdims=True)
    b_tot = b_row[:, L - 1:L] if not rev else b_row[:, 0:1]

    m = m_ref[...]
    c_aug = c_ref[...]
    log_w = jnp.where(causal, b_col - b_row + i_row, NEG)
    inter = b_col + m
    m_t = jnp.maximum(inter, jnp.max(log_w, axis=1, keepdims=True))
    w_inter = jnp.exp(inter - m_t)
    w_intra = jnp.exp(log_w - m_t) * _dot_nt(q, k.astype(BF16))
    tot = w_inter * _dot(q, c_aug.astype(BF16)) + _dot(w_intra.astype(BF16), vaug)
    num = tot[:, :HEAD_W]
    den = tot[:, HEAD_W:]
    h_ref[rows, :] = num / jnp.maximum(jnp.abs(den), jnp.exp(-m_t))

    upd_col = b_tot - b_col + i_col
    upd_row = b_tot - b_row + i_row
    m_new = jnp.maximum(b_tot + m, jnp.max(upd_row, axis=1, keepdims=True))
    w_s = jnp.exp(b_tot + m - m_new)
    w_u = jnp.exp(upd_col - m_new)
    c_ref[...] = w_s * c_aug + _dot_tn((w_u * k).astype(BF16), vaug)
    m_ref[...] = m_new


def _mlstm_kernel(fb_ref, q_ref, k_ref, v_ref, og_ref, gc_ref, gr_ref, nw_ref, y_ref,
                  hf_ref, hb_ref, cf_ref, cb_ref, mf_ref, mb_ref, *, n_ctx, n_tok):
    L = MLSTM_CHUNK
    nc, ncc = n_tok // L, n_ctx // L
    h = pl.program_id(1)
    fb_f = fb_ref[0, h]
    fb_b = fb_ref[1, h]
    for ref in (cf_ref, cb_ref, mf_ref, mb_ref):
        ref[...] = jnp.zeros_like(ref)

    def body(step, carry):
        _mlstm_chunk(q_ref, k_ref, v_ref, gc_ref, gr_ref, fb_f, cf_ref, mf_ref, hf_ref,
                     pl.multiple_of(step * L, L), False)
        cb = jnp.where(step < ncc, ncc - 1 - step, nc - 1 - (step - ncc))
        _mlstm_chunk(q_ref, k_ref, v_ref, gc_ref, gr_ref, fb_b, cb_ref, mb_ref, hb_ref,
                     pl.multiple_of(cb * L, L), True)
        return carry

    lax.fori_loop(0, nc, body, 0)

    fr = 256
    def fin(i, carry):
        rows = pl.ds(pl.multiple_of(i * fr, fr), fr)
        o = jax.nn.sigmoid(og_ref[rows, :]) * (hf_ref[rows, :] + hb_ref[rows, :])
        y_ref[rows, :] = (_rms(o) * nw_ref[...]).astype(y_ref.dtype)
        return carry

    lax.fori_loop(0, n_tok // fr, fin, 0)


def _mlstm(p, gcol, grow, f_bias, norm_w, l, n_ctx):
    b, t, _ = p.shape
    base = 11 * HEADS

    def col(seg):
        return pl.BlockSpec((None, t, HEAD_W), lambda bi, h: (bi, 0, base + seg * HEADS + h))

    return pl.pallas_call(
        functools.partial(_mlstm_kernel, n_ctx=n_ctx, n_tok=t),
        grid=(b, HEADS),
        in_specs=[pl.BlockSpec(memory_space=pltpu.SMEM),
                  col(0), col(1), col(2), col(3),
                  pl.BlockSpec((None, None, t, 8), lambda bi, h: (bi, h, 0, 0)),
                  pl.BlockSpec((None, None, 8, t), lambda bi, h: (bi, h, 0, 0)),
                  pl.BlockSpec((None, 1, HEAD_W), lambda bi, h: (l, 0, h))],
        out_specs=pl.BlockSpec((None, t, HEAD_W), lambda bi, h: (bi, 0, h)),
        out_shape=jax.ShapeDtypeStruct((b, t, GROUP_W), BF16),
        scratch_shapes=[pltpu.VMEM((t, HEAD_W), F32), pltpu.VMEM((t, HEAD_W), F32),
                        pltpu.VMEM((HEAD_W, 2 * HEAD_W), F32), pltpu.VMEM((HEAD_W, 2 * HEAD_W), F32),
                        pltpu.VMEM((1, 1), F32), pltpu.VMEM((1, 1), F32)],
        compiler_params=_cparams("arbitrary", "arbitrary"),
    )(f_bias, p, p, p, p, gcol, grow, norm_w)


def _outproj_kernel(ya_ref, yb_ref, yc_ref, yd_ref, w_ref, x_ref, mx_ref, mc_ref, o_ref, *, n_ctx, tm):
    i = pl.program_id(1)
    acc = _dot(ya_ref[...], w_ref[0])
    acc += _dot(yb_ref[...], w_ref[1])
    acc += _dot(yc_ref[...], w_ref[2])
    acc += _dot(yd_ref[...], w_ref[3])
    rows = i * tm + lax.broadcasted_iota(jnp.int32, (tm, 1), 0)
    gate = jnp.where(rows < n_ctx, mc_ref[2:3, :], mx_ref[2:3, :])
    o_ref[...] = x_ref[...] + gate * acc


def _outproj(ys, w_out, xs, mods, l, n_ctx):
    b, t, d = xs.shape
    tm, tn = ROW_TILE, 1024
    nb = mods.shape[1] - 1
    yspec = pl.BlockSpec((None, tm, GROUP_W), lambda bi, i, j: (bi, i, 0))
    return pl.pallas_call(
        functools.partial(_outproj_kernel, n_ctx=n_ctx, tm=tm),
        grid=(b, t // tm, d // tn),
        in_specs=[yspec, yspec, yspec, yspec,
                  pl.BlockSpec((None, 4, GROUP_W, tn), lambda bi, i, j: (l, 0, 0, j)),
                  pl.BlockSpec((None, tm, tn), lambda bi, i, j: (bi, i, j)),
                  pl.BlockSpec((None, None, 6, tn), lambda bi, i, j: (l, bi, 0, j)),
                  pl.BlockSpec((None, None, 6, tn), lambda bi, i, j: (l, nb, 0, j))],
        out_specs=pl.BlockSpec((None, tm, tn), lambda bi, i, j: (bi, i, j)),
        out_shape=jax.ShapeDtypeStruct((b, t, d), F32),
        compiler_params=_cparams("arbitrary", "arbitrary", "arbitrary"),
    )(*ys, w_out, xs, mods, mods)


def _ffn_kernel(xm_ref, xp_ref, xn_ref, mx_ref, mc_ref, wa_ref, wg_ref, cwa_ref, cwg_ref,
                cba_ref, cbg_ref, wd_ref, o_ref, hs_ref, ua_ref, ug_ref, *, n_ctx, n_tok, tm):
    i = pl.program_id(1)
    f = pl.program_id(2)
    row0 = i * tm

    @pl.when(f == 0)
    def _():
        hs_ref[0:HALO, :] = _modulate(xp_ref[...], mx_ref, mc_ref, row0 - HALO, 3, n_ctx).astype(BF16)
        _modulate_into(hs_ref, HALO, xm_ref, mx_ref, mc_ref, row0, 3, n_ctx)
        hs_ref[HALO + tm:, :] = _modulate(xn_ref[...], mx_ref, mc_ref, row0 + tm, 3, n_ctx).astype(BF16)

    ua_ref[...] = _dot(hs_ref[...], wa_ref[...])
    ug_ref[...] = _dot(hs_ref[...], wg_ref[...])

    @pl.when(f == 0)
    def _():
        o_ref[...] = jnp.zeros_like(o_ref)

    for r in range(0, tm, FFN_SUB):
        rows = row0 + r + lax.broadcasted_iota(jnp.int32, (FFN_SUB, 1), 0)
        has_prev = (rows != 0) & (rows != n_ctx)
        has_next = (rows != n_ctx - 1) & (rows != n_tok - 1)

        def conv(u_ref, cw_ref, cb_ref):
            lo = HALO + r
            prev = jnp.where(has_prev, u_ref[lo - 1:lo - 1 + FFN_SUB, :], 0.0)
            nxt = jnp.where(has_next, u_ref[lo + 1:lo + 1 + FFN_SUB, :], 0.0)
            return (prev * cw_ref[0:1, :] + u_ref[lo:lo + FFN_SUB, :] * cw_ref[1:2, :]
                    + nxt * cw_ref[2:3, :] + cb_ref[...])

        act = (_silu(conv(ua_ref, cwa_ref, cba_ref)) * conv(ug_ref, cwg_ref, cbg_ref)).astype(BF16)
        o_ref[r:r + FFN_SUB, :] += _dot(act, wd_ref[...])

    @pl.when(f == pl.num_programs(2) - 1)
    def _():
        rows = row0 + lax.broadcasted_iota(jnp.int32, (tm, 1), 0)
        gate = jnp.where(rows < n_ctx, mc_ref[5:6, :], mx_ref[5:6, :])
        o_ref[...] = xm_ref[...] + gate * o_ref[...]


def _ffn(xs, mods, w_up, conv_w, conv_b, w_down, l, n_ctx):
    b, t, d = xs.shape
    tm, tf = ROW_TILE, FF_TILE
    d_ff = w_down.shape[1]
    nf = d_ff // tf
    nb = mods.shape[1] - 1
    hb = tm // HALO
    last = t // HALO - 1
    return pl.pallas_call(
        functools.partial(_ffn_kernel, n_ctx=n_ctx, n_tok=t, tm=tm),
        grid=(b, t // tm, nf),
        in_specs=[
            pl.BlockSpec((None, tm, d), lambda bi, i, f: (bi, i, 0)),
            pl.BlockSpec((None, HALO, d), lambda bi, i, f: (bi, jnp.maximum(i * hb - 1, 0), 0)),
            pl.BlockSpec((None, HALO, d), lambda bi, i, f: (bi, jnp.minimum((i + 1) * hb, last), 0)),
            pl.BlockSpec((None, None, 6, d), lambda bi, i, f: (l, bi, 0, 0)),
            pl.BlockSpec((None, None, 6, d), lambda bi, i, f: (l, nb, 0, 0)),
            pl.BlockSpec((None, d, tf), lambda bi, i, f: (l, 0, f)),
            pl.BlockSpec((None, d, tf), lambda bi, i, f: (l, 0, nf + f)),
            pl.BlockSpec((None, CONV_W, tf), lambda bi, i, f: (l, 0, f)),
            pl.BlockSpec((None, CONV_W, tf), lambda bi, i, f: (l, 0, nf + f)),
            pl.BlockSpec((None, 1, tf), lambda bi, i, f: (l, 0, f)),
            pl.BlockSpec((None, 1, tf), lambda bi, i, f: (l, 0, nf + f)),
            pl.BlockSpec((None, tf, d), lambda bi, i, f: (l, f, 0)),
        ],
        out_specs=pl.BlockSpec((None, tm, d), lambda bi, i, f: (bi, i, 0)),
        out_shape=jax.ShapeDtypeStruct((b, t, d), F32),
        scratch_shapes=[pltpu.VMEM((tm + 2 * HALO, d), BF16),
                        pltpu.VMEM((tm + 2 * HALO, tf), F32),
                        pltpu.VMEM((tm + 2 * HALO, tf), F32)],
        compiler_params=_cparams("arbitrary", "arbitrary", "arbitrary"),
    )(xs, xs, xs, mods, mods, w_up, w_up, conv_w, conv_w, conv_b, conv_b, w_down)


def _final_kernel(x_ref, w_ref, o_ref):
    o_ref[...] = _rms(x_ref[...]) * w_ref[...]


def _final_norm(xs, w, n_ctx):
    b, t, d = xs.shape
    tr = n_ctx
    off = n_ctx // tr
    return pl.pallas_call(
        _final_kernel,
        grid=(b, (t - n_ctx) // tr),
        in_specs=[pl.BlockSpec((None, tr, d), lambda bi, i: (bi, i + off, 0)),
                  pl.BlockSpec((1, d), lambda bi, i: (0, 0))],
        out_specs=pl.BlockSpec((None, tr, d), lambda bi, i: (bi, i, 0)),
        out_shape=jax.ShapeDtypeStruct((b, t - n_ctx, d), F32),
        compiler_params=_cparams("arbitrary", "arbitrary"),
    )(xs, w.reshape(1, d))


def _lambda_init(layer_idx):
    return 0.8 - 0.6 * math.exp(-0.3 * layer_idx)


def _rope_tables(n_ctx, n_lat):
    dh = HEAD_W // 2
    t = np.arange(n_lat)
    row = (t // GRID_W).astype(np.float32)
    colp = (t % GRID_W).astype(np.float32)
    half = dh // 2
    inv = (ROPE_THETA ** (-np.arange(0, half, 2, dtype=np.float32) / half)).astype(np.float32)
    ar = row[:, None] * inv
    ac = colp[:, None] * inv
    ang = jnp.asarray(np.concatenate([ar, ar, ac, ac], axis=-1))
    cos, sin = jnp.cos(ang), jnp.sin(ang)
    sign = np.where((np.arange(dh) % 32) < 16, -1.0, 1.0).astype(np.float32)
    cos = jnp.concatenate([jnp.ones((n_ctx, dh), F32), cos], axis=0)
    sin = jnp.concatenate([jnp.zeros((n_ctx, dh), F32), sin * sign], axis=0)
    return jnp.tile(cos, (1, 2)), jnp.tile(sin, (1, 2))


def _na_bias(na_rpb, kh):
    w = np.arange(GRID_W)
    col_start = np.clip(w - NA_KW // 2, 0, GRID_W - NA_KW)
    col_ok = (w[None, :] >= col_start[:, None]) & (w[None, :] < col_start[:, None] + NA_KW)
    coff = np.clip(w[None, :] - w[:, None], -(NA_KW - 1), NA_KW - 1) + (NA_KW - 1)
    roff = np.arange(kh)[None, :] - np.arange(kh)[:, None] + (NA_KH - 1)
    bias = na_rpb.astype(F32)[:, :, roff[:, None, :, None], coff[None, :, None, :]]
    bias = jnp.where(col_ok[None, None, None, :, None, :], bias, NEG)
    return bias.reshape(na_rpb.shape[0], na_rpb.shape[1], kh, GRID_W, kh * GRID_W)


def kernel(x, c, ctx, c_ctx, w_ada, b_ada, w_in, hgrn_lb, hgrn_norm_w, diff_lam, diff_norm_w, na_rpb,
           mlstm_f_bias, mlstm_norm_w, w_out, w_up, conv_w, conv_b, w_down, final_norm_w):
    bsz, n_lat, d = x.shape
    n_ctx = ctx.shape[1]
    depth = w_in.shape[0]
    d_ff = w_down.shape[1]
    grid_rows = n_lat // GRID_W

    xs = jnp.concatenate([ctx, x], axis=1)
    n_mod = -(-(bsz + 1) // 8) * 8
    cc = jnp.zeros((n_mod, d), F32).at[:bsz].set(c).at[n_mod - 1].set(c_ctx)
    mods = _ada(cc, w_ada, b_ada).reshape(depth, n_mod, 6, d)

    w_in_b = w_in[:, :, :D_MAIN].astype(BF16)
    w_gate_b = jnp.pad(w_in[:, :, D_MAIN:], ((0, 0), (0, 0), (0, HEAD_W - N_GATES))).astype(BF16)
    w_out_b = w_out.astype(BF16).reshape(depth, 4, GROUP_W, d)
    w_up_b = w_up.astype(BF16)
    w_down_b = w_down.astype(BF16)
    conv_b3 = conv_b.reshape(depth, 1, 2 * d_ff)

    lb = jnp.cumsum(jax.nn.softmax(hgrn_lb.astype(F32), axis=0), axis=0)
    lb = lb - lb[:1]
    zero = jnp.zeros_like(lb[:, :1])
    gate_par = jnp.concatenate([jnp.log(lb[:, 0:1]), jnp.log1p(-lb[:, 0:1]), 1.0 - lb[:, 0:1],
                                jnp.log(lb[:, 1:2]), jnp.log1p(-lb[:, 1:2]), 1.0 - lb[:, 1:2],
                                zero, zero], axis=1)
    cos, sin_signed = _rope_tables(n_ctx, n_lat)
    lam = jnp.pad(diff_lam.astype(F32), ((0, 0), (0, 4), (0, HEAD_W - diff_lam.shape[2])))
    bias = _na_bias(na_rpb, min(NA_KH, grid_rows))
    f_bias = mlstm_f_bias.astype(F32)

    def nw(w):
        return w.astype(F32).reshape(depth, 1, GROUP_W)

    hgrn_nw, diff_nw, mlstm_nw = nw(hgrn_norm_w), nw(diff_norm_w), nw(mlstm_norm_w)

    for l in range(depth):
        p, gates = _inproj(xs, mods, w_in_b, w_gate_b, l, n_ctx)
        g4 = gates[:, :, :N_GATES].reshape(bsz, -1, 4, HEADS)
        gcol = jnp.pad(g4.transpose(0, 3, 1, 2), ((0, 0), (0, 0), (0, 0), (0, 4)))
        grow = jnp.pad(g4.transpose(0, 3, 2, 1), ((0, 0), (0, 0), (0, 4), (0, 0)))
        ya = _hgrn(p, gate_par, hgrn_nw, l, n_ctx)
        yb = _diff(p, cos, sin_signed, lam, diff_nw, l, n_ctx, _lambda_init(l))
        yc = _na(p, bias, l, n_ctx)
        yd = _mlstm(p, gcol, grow, f_bias[l], mlstm_nw, l, n_ctx)
        xs = _outproj((ya, yb, yc, yd), w_out_b, xs, mods, l, n_ctx)
        xs = _ffn(xs, mods, w_up_b, conv_w, conv_b3, w_down_b, l, n_ctx)
    return _final_norm(xs, final_norm_w, n_ctx)
```

```python
import functools
import math

import numpy as np
import jax
import jax.numpy as jnp
from jax import lax
from jax.experimental import pallas as pl
from jax.experimental.pallas import tpu as pltpu

F32 = jnp.float32
BF16 = jnp.bfloat16

HEADS = 4
HEAD_W = 128
GROUP_W = HEADS * HEAD_W
GRID_W = 64
NA_KH = 8
NA_KW = 16
ROPE_THETA = 10000.0
EPS = 1e-6
CONV_W = 3
D_MAIN = 15 * GROUP_W
N_GATES = 4 * HEADS

VMEM_LIMIT_BYTES = 56 * 1024 * 1024
NEG = -1e30
LOG2E = 1.4426950408889634

HGRN_CHUNK = 64
HGRN_SUB = 16
HGRN_UNROLL = 2
MLSTM_CHUNK = 128
MLSTM_UNROLL = 2
NA_UNROLL = 4
ROW_TILE = 768
FF_TILE = 512
HALO = 16
FFN_SUB = 256
MOD_SUB = 256


def _cparams(*sem):
    return pltpu.CompilerParams(dimension_semantics=sem, vmem_limit_bytes=VMEM_LIMIT_BYTES)


def _dot(a, b):
    return jnp.dot(a, b, preferred_element_type=F32)


def _dot_nt(a, b):
    return lax.dot_general(a, b, (((1,), (1,)), ((), ())), preferred_element_type=F32)


def _dot_tn(a, b):
    return lax.dot_general(a, b, (((0,), (0,)), ((), ())), preferred_element_type=F32)


def _silu(t):
    return t * jax.nn.sigmoid(t)


def _log_sigmoid(z):
    return jnp.minimum(z, 0.0) - jnp.log1p(jnp.exp(-jnp.abs(z)))


def _rms(t):
    return t * lax.rsqrt(jnp.mean(t * t, axis=-1, keepdims=True) + EPS)


def _split3(t):
    hi = t.astype(BF16)
    r1 = t - hi.astype(F32)
    mid = r1.astype(BF16)
    lo = (r1 - mid.astype(F32)).astype(BF16)
    return hi, mid, lo


def _modulate(x, mx_ref, mc_ref, row0, k, n_ctx):
    rows = row0 + lax.broadcasted_iota(jnp.int32, (x.shape[0], 1), 0)
    is_ctx = rows < n_ctx
    shift = jnp.where(is_ctx, mc_ref[k:k + 1, :], mx_ref[k:k + 1, :])
    scale = jnp.where(is_ctx, mc_ref[k + 1:k + 2, :], mx_ref[k + 1:k + 2, :])
    return _rms(x) * (1.0 + scale) + shift


def _modulate_into(dst_ref, dst_off, x_ref, mx_ref, mc_ref, row0, k, n_ctx):
    n = x_ref.shape[0]
    sub = math.gcd(n, MOD_SUB)
    for r in range(0, n, sub):
        h = _modulate(x_ref[r:r + sub, :], mx_ref, mc_ref, row0 + r, k, n_ctx)
        dst_ref[dst_off + r:dst_off + r + sub, :] = h.astype(dst_ref.dtype)


def _ada_kernel(c_ref, w_ref, b_ref, o_ref):
    s = _silu(c_ref[...]).astype(BF16)
    o_ref[...] = _dot(s, w_ref[...].astype(BF16)) + b_ref[...]


def _ada(cc, w_ada, b_ada):
    depth, d, n = w_ada.shape
    tn = 1024
    return pl.pallas_call(
        _ada_kernel,
        grid=(depth, n // tn),
        in_specs=[
            pl.BlockSpec((8, d), lambda l, j: (0, 0)),
            pl.BlockSpec((None, d, tn), lambda l, j: (l, 0, j)),
            pl.BlockSpec((None, 1, tn), lambda l, j: (l, 0, j)),
        ],
        out_specs=pl.BlockSpec((None, 8, tn), lambda l, j: (l, 0, j)),
        out_shape=jax.ShapeDtypeStruct((depth, 8, n), F32),
        compiler_params=_cparams("arbitrary", "arbitrary"),
    )(cc, w_ada, b_ada.reshape(depth, 1, n))


def _inproj_kernel(x_ref, mx_ref, mc_ref, w_ref, wg_ref, p_ref, gate_ref, h_ref, *, n_ctx, tm):
    i = pl.program_id(1)
    j = pl.program_id(2)

    @pl.when(j == 0)
    def _():
        _modulate_into(h_ref, 0, x_ref, mx_ref, mc_ref, i * tm, 0, n_ctx)
        gate_ref[...] = _dot(h_ref[...], wg_ref[...])

    p_ref[...] = _dot(h_ref[...], w_ref[...])


def _inproj(xs, mods, w_in, w_gate, l, n_ctx):
    b, t, d = xs.shape
    tm, tn = ROW_TILE, GROUP_W
    nb = mods.shape[1] - 1
    return pl.pallas_call(
        functools.partial(_inproj_kernel, n_ctx=n_ctx, tm=tm),
        grid=(b, t // tm, D_MAIN // tn),
        in_specs=[
            pl.BlockSpec((None, tm, d), lambda bi, i, j: (bi, i, 0)),
            pl.BlockSpec((None, None, 6, d), lambda bi, i, j: (l, bi, 0, 0)),
            pl.BlockSpec((None, None, 6, d), lambda bi, i, j: (l, nb, 0, 0)),
            pl.BlockSpec((None, d, tn), lambda bi, i, j: (l, 0, j)),
            pl.BlockSpec((None, d, HEAD_W), lambda bi, i, j: (l, 0, 0)),
        ],
        out_specs=[
            pl.BlockSpec((None, tm, tn), lambda bi, i, j: (bi, i, j)),
            pl.BlockSpec((None, tm, HEAD_W), lambda bi, i, j: (bi, i, 0)),
        ],
        out_shape=[
            jax.ShapeDtypeStruct((b, t, D_MAIN), F32),
            jax.ShapeDtypeStruct((b, t, HEAD_W), F32),
        ],
        scratch_shapes=[pltpu.VMEM((tm, d), BF16)],
        compiler_params=_cparams("arbitrary", "arbitrary", "arbitrary"),
    )(xs, mods, mods, w_in, w_gate)


def _round_robin(gens):
    live = list(gens)
    while live:
        nxt = []
        for g in live:
            try:
                next(g)
                nxt.append(g)
            except StopIteration:
                pass
        live = nxt


def _hgrn_chunk(q_ref, v_ref, z_ref, gp_ref, state, o_ref, r0, rev):
    L, SC = HGRN_CHUNK, HGRN_SUB
    nsub = L // SC
    rows = pl.ds(r0, L)
    q = _silu(q_ref[rows, :]) * (HEAD_W ** -0.5)
    z = z_ref[rows, :]
    v = v_ref[rows, :].astype(BF16)
    e = jnp.exp(-jnp.abs(z))
    inv = 1.0 / (1.0 + e)
    a = gp_ref[1:2, :] + (jnp.minimum(z, 0.0) - jnp.log(1.0 + e))
    ll = gp_ref[0:1, :]
    lf = jnp.maximum(ll, a) + jnp.log(1.0 + jnp.exp(-jnp.abs(ll - a)))
    k = gp_ref[2:3, :] * (jnp.where(z >= 0.0, e, 1.0) * inv)

    t_i = lax.broadcasted_iota(jnp.int32, (L, L), 0)
    s_i = lax.broadcasted_iota(jnp.int32, (L, L), 1)
    t_blk = t_i & ~(SC - 1)
    if not rev:
        local = (s_i <= t_i) & (s_i >= t_blk)
        causal = s_i <= t_i
    else:
        local = (s_i >= t_i) & (s_i < t_blk + SC)
        causal = s_i >= t_i
    sel = jnp.where(local, 1.0, 0.0).astype(BF16)
    hi, mid, lo = _split3(lf)
    bl = _dot(sel, hi) + _dot(sel, mid) + _dot(sel, lo)
    yield

    def blk(x, i):
        return x[i * SC:(i + 1) * SC]

    order = list(range(nsub)) if not rev else list(range(nsub - 1, -1, -1))
    r = [None] * nsub
    acc = jnp.zeros((1, HEAD_W), F32)
    for i in order:
        r[i] = acc
        acc = acc + (bl[(i + 1) * SC - 1:(i + 1) * SC] if not rev else bl[i * SC:i * SC + 1])
    b_end = acc

    qt = q * jnp.exp(bl)
    kinv = k * jnp.exp(-bl)
    kd = jnp.concatenate([blk(kinv, j) * jnp.exp(b_end - r[j]) for j in range(nsub)], axis=0).astype(BF16)
    upd = _dot_tn(v, kd)
    zero = jnp.zeros((SC, HEAD_W), F32)
    parts = []
    for i in range(nsub):
        seen = [j for j in range(nsub) if (j <= i if not rev else j >= i)]
        kk = jnp.concatenate(
            [(blk(kinv, j) if j == i else blk(kinv, j) * jnp.exp(r[i] - r[j])) if j in seen else zero
             for j in range(nsub)], axis=0).astype(BF16)
        parts.append(_dot_nt(blk(qt, i).astype(BF16), kk))
    qb = jnp.concatenate([blk(qt, i) * jnp.exp(r[i]) for i in range(nsub)], axis=0).astype(BF16)
    yield

    st = state[0]
    amat = jnp.where(causal, jnp.concatenate(parts, axis=0), 0.0)
    o_ref[rows, :] = _dot_nt(qb, st.astype(BF16)) + _dot(amat.astype(BF16), v)
    state[0] = st * jnp.exp(b_end) + upd


def _hgrn_kernel(q_ref, v_ref, zf_ref, zb_ref, g_ref, gp_ref, nw_ref, y_ref,
                 of_ref, ob_ref, sf_ref, sb_ref, *, n_ctx, n_tok):
    L, U = HGRN_CHUNK, HGRN_UNROLL
    nc, ncc = n_tok // L, n_ctx // L
    sf_ref[...] = jnp.zeros_like(sf_ref)
    sb_ref[...] = jnp.zeros_like(sb_ref)

    def body(it, carry):
        sf = [sf_ref[...]]
        sb = [sb_ref[...]]
        gens = []
        for u in range(U):
            step = it * U + u
            gens.append(_hgrn_chunk(q_ref, v_ref, zf_ref, gp_ref.at[0:3], sf, of_ref,
                                    pl.multiple_of(step * L, L), False))
            cb = jnp.where(step < ncc, ncc - 1 - step, nc - 1 - (step - ncc))
            gens.append(_hgrn_chunk(q_ref, v_ref, zb_ref, gp_ref.at[3:6], sb, ob_ref,
                                    pl.multiple_of(cb * L, L), True))
        _round_robin(gens)
        sf_ref[...] = sf[0]
        sb_ref[...] = sb[0]
        return carry

    lax.fori_loop(0, nc // U, body, 0)

    fr = 256
    def fin(i, carry):
        rows = pl.ds(pl.multiple_of(i * fr, fr), fr)
        o = of_ref[rows, :] + ob_ref[rows, :]
        y_ref[rows, :] = (_rms(o) * nw_ref[...] * _silu(g_ref[rows, :])).astype(y_ref.dtype)
        return carry

    lax.fori_loop(0, n_tok // fr, fin, 0)


def _hgrn(p, gate_par, norm_w, l, n_ctx):
    b, t, _ = p.shape
    nh = GROUP_W // HEAD_W

    def col(seg):
        return pl.BlockSpec((None, t, HEAD_W), lambda bi, h: (bi, 0, seg * nh + h))

    return pl.pallas_call(
        functools.partial(_hgrn_kernel, n_ctx=n_ctx, n_tok=t),
        grid=(b, HEADS),
        in_specs=[col(0), col(1), col(2), col(3), col(4),
                  pl.BlockSpec((None, 8, HEAD_W), lambda bi, h: (l, 0, h)),
                  pl.BlockSpec((None, 1, HEAD_W), lambda bi, h: (l, 0, h))],
        out_specs=pl.BlockSpec((None, t, HEAD_W), lambda bi, h: (bi, 0, h)),
        out_shape=jax.ShapeDtypeStruct((b, t, GROUP_W), BF16),
        scratch_shapes=[pltpu.VMEM((t, HEAD_W), F32), pltpu.VMEM((t, HEAD_W), F32),
                        pltpu.VMEM((HEAD_W, HEAD_W), F32), pltpu.VMEM((HEAD_W, HEAD_W), F32)],
        compiler_params=_cparams("arbitrary", "arbitrary"),
    )(p, p, p, p, p, gate_par, norm_w)


def _rope(t, cos, sin_signed):
    lane = lax.broadcasted_iota(jnp.int32, t.shape, 1)
    first = (lane & 31) < 16
    rot = jnp.where(first, pltpu.roll(t, HEAD_W - 16, 1), pltpu.roll(t, 16, 1))
    return t * cos + rot * sin_signed


def _softmax_rows(s):
    e = jnp.exp(s - jnp.max(s, axis=-1, keepdims=True))
    return e * (1.0 / jnp.sum(e, axis=-1, keepdims=True))


def _diff_kernel(q_ref, k_ref, v_ref, cos_ref, sin_ref, lam_ref, nw_ref, y_ref, kb_ref, vb_ref,
                 *, n_ctx, tq, lam_init):
    qi = pl.program_id(2)

    @pl.when(qi == 0)
    def _():
        kb_ref[...] = _rope(k_ref[...], cos_ref[...], sin_ref[...]).astype(BF16)
        v = v_ref[...]
        vb_ref[...] = jnp.concatenate([v, jnp.ones_like(v)], axis=1).astype(BF16)

    lam = lam_ref[...]
    lam_full = (jnp.exp(jnp.sum(lam[0:1] * lam[1:2], axis=-1, keepdims=True))
                - jnp.exp(jnp.sum(lam[2:3] * lam[3:4], axis=-1, keepdims=True)) + lam_init)
    rows = pl.ds(pl.multiple_of(qi * tq, tq), tq)
    q = _rope(q_ref[...], cos_ref[rows, :], sin_ref[rows, :]) * ((HEAD_W // 2) ** -0.5 * LOG2E)
    lane = lax.broadcasted_iota(jnp.int32, q.shape, 1)
    q0 = jnp.where(lane < HEAD_W // 2, q, 0.0).astype(BF16)
    q1 = jnp.where(lane >= HEAD_W // 2, q, 0.0).astype(BF16)

    def attend(kb, vb):
        s0 = _dot_nt(q0, kb)
        s1 = _dot_nt(q1, kb)
        e0 = jnp.exp2(s0 - jnp.max(s0, axis=-1, keepdims=True)).astype(BF16)
        e1 = jnp.exp2(s1 - jnp.max(s1, axis=-1, keepdims=True)).astype(BF16)
        t0 = _dot(e0, vb)
        t1 = _dot(e1, vb)
        o = t0[:, :HEAD_W] / t0[:, HEAD_W:] - lam_full * (t1[:, :HEAD_W] / t1[:, HEAD_W:])
        y_ref[...] = (_rms(o) * nw_ref[...] * (1.0 - lam_init)).astype(y_ref.dtype)

    @pl.when(qi == 0)
    def _():
        attend(kb_ref[0:n_ctx, :], vb_ref[0:n_ctx, :])

    @pl.when(qi > 0)
    def _():
        attend(kb_ref[...], vb_ref[...])


def _diff(p, cos, sin_signed, lam, norm_w, l, n_ctx, lam_init):
    b, t, _ = p.shape
    tq = n_ctx
    base = 5 * HEADS

    def col(seg, rows, rmap):
        return pl.BlockSpec((None, rows, HEAD_W), lambda bi, h, qi: (bi, rmap(qi), base + seg * HEADS + h))

    return pl.pallas_call(
        functools.partial(_diff_kernel, n_ctx=n_ctx, tq=tq, lam_init=lam_init),
        grid=(b, HEADS, t // tq),
        in_specs=[col(0, tq, lambda qi: qi), col(1, t, lambda qi: 0), col(2, t, lambda qi: 0),
                  pl.BlockSpec((t, HEAD_W), lambda bi, h, qi: (0, 0)),
                  pl.BlockSpec((t, HEAD_W), lambda bi, h, qi: (0, 0)),
                  pl.BlockSpec((None, 8, HEAD_W), lambda bi, h, qi: (l, 0, 0)),
                  pl.BlockSpec((None, 1, HEAD_W), lambda bi, h, qi: (l, 0, h))],
        out_specs=pl.BlockSpec((None, tq, HEAD_W), lambda bi, h, qi: (bi, qi, h)),
        out_shape=jax.ShapeDtypeStruct((b, t, GROUP_W), BF16),
        scratch_shapes=[pltpu.VMEM((t, HEAD_W), BF16), pltpu.VMEM((t, 2 * HEAD_W), BF16)],
        compiler_params=_cparams("arbitrary", "arbitrary", "arbitrary"),
    )(p, p, p, cos, sin_signed, lam, norm_w)


def _na_kernel(q_ref, k_ref, v_ref, bias_ref, y_ref, kb_ref, vb_ref, *, n_ctx, grid_rows, kh):
    scale = HEAD_W ** -0.5
    kb_ref[...] = k_ref[...].astype(BF16)
    vb_ref[...] = v_ref[...].astype(BF16)
    kc = kb_ref[0:n_ctx, :]
    vc = vb_ref[0:n_ctx, :]

    qc = (q_ref[0:n_ctx, :] * scale).astype(BF16)
    pc = _softmax_rows(_dot_nt(qc, kc))
    y_ref[0:n_ctx, :] = _dot(pc.astype(BF16), vc).astype(y_ref.dtype)

    win = kh * GRID_W

    def one_row(r, out):
        rows = pl.ds(pl.multiple_of(n_ctx + r * GRID_W, GRID_W), GRID_W)
        q = (q_ref[rows, :] * scale).astype(BF16)
        start = jnp.clip(r - kh // 2, 0, grid_rows - kh)
        wrows = pl.ds(pl.multiple_of(n_ctx + start * GRID_W, GRID_W), win)
        sb = _dot_nt(q, kb_ref[wrows, :]) + bias_ref[r - start]
        sc = _dot_nt(q, kc)
        yield
        m = jnp.maximum(jnp.max(sb, axis=-1, keepdims=True), jnp.max(sc, axis=-1, keepdims=True))
        eb = jnp.exp(sb - m)
        ec = jnp.exp(sc - m)
        inv = 1.0 / (jnp.sum(eb, axis=-1, keepdims=True) + jnp.sum(ec, axis=-1, keepdims=True))
        pb = (eb * inv).astype(BF16)
        pc_ = (ec * inv).astype(BF16)
        yield
        out.append(_dot(pb, vb_ref[wrows, :]) + _dot(pc_, vc))

    def body(it, carry):
        outs = [[] for _ in range(NA_UNROLL)]
        _round_robin([one_row(it * NA_UNROLL + u, outs[u]) for u in range(NA_UNROLL)])
        o = jnp.concatenate([x[0] for x in outs], axis=0)
        rows = pl.ds(pl.multiple_of(n_ctx + it * (NA_UNROLL * GRID_W), NA_UNROLL * GRID_W), NA_UNROLL * GRID_W)
        y_ref[rows, :] = o.astype(y_ref.dtype)
        return carry

    lax.fori_loop(0, grid_rows // NA_UNROLL, body, 0)


def _na(p, bias, l, n_ctx):
    b, t, _ = p.shape
    grid_rows = (t - n_ctx) // GRID_W
    kh = min(NA_KH, grid_rows)
    base = 8 * HEADS

    def col(seg):
        return pl.BlockSpec((None, t, HEAD_W), lambda bi, h: (bi, 0, base + seg * HEADS + h))

    return pl.pallas_call(
        functools.partial(_na_kernel, n_ctx=n_ctx, grid_rows=grid_rows, kh=kh),
        grid=(b, HEADS),
        in_specs=[col(0), col(1), col(2),
                  pl.BlockSpec((None, None, kh, GRID_W, kh * GRID_W), lambda bi, h: (l, h, 0, 0, 0))],
        out_specs=pl.BlockSpec((None, t, HEAD_W), lambda bi, h: (bi, 0, h)),
        out_shape=jax.ShapeDtypeStruct((b, t, GROUP_W), BF16),
        scratch_shapes=[pltpu.VMEM((t, HEAD_W), BF16), pltpu.VMEM((t, HEAD_W), BF16)],
        compiler_params=_cparams("arbitrary", "arbitrary"),
    )(p, p, p, bias)


def _mlstm_chunk(q_ref, k_ref, v_ref, gc_ref, gr_ref, fb, state, h_ref, r0, rev):
    L = MLSTM_CHUNK
    rows = pl.ds(r0, L)
    gi, gf = (2, 3) if rev else (0, 1)
    q = (q_ref[rows, :] * (HEAD_W ** -0.5)).astype(BF16)
    k = k_ref[rows, :]
    v = v_ref[rows, :]
    vaug = jnp.concatenate([v, jnp.ones_like(v)], axis=1).astype(BF16)
    gcol = gc_ref[rows, :]
    grow = gr_ref[:, rows]
    i_col = gcol[:, gi:gi + 1]
    f_col = _log_sigmoid(gcol[:, gf:gf + 1] + fb)
    i_row = grow[gi:gi + 1, :]
    f_row = _log_sigmoid(grow[gf:gf + 1, :] + fb)

    t_i = lax.broadcasted_iota(jnp.int32, (L, L), 0)
    s_i = lax.broadcasted_iota(jnp.int32, (L, L), 1)
    causal = (s_i <= t_i) if not rev else (s_i >= t_i)
    causal_t = (t_i <= s_i) if not rev else (t_i >= s_i)
    b_col = jnp.sum(jnp.where(causal, f_row, 0.0), axis=1, keepdims=True)
    b_row = jnp.sum(jnp.where(causal_t, f_col, 0.0), axis=0, keepdims=True)
    b_tot = b_row[:, L - 1:L] if not rev else b_row[:, 0:1]

    log_w = jnp.where(causal, b_col - b_row + i_row, NEG)
    row_max = jnp.max(log_w, axis=1, keepdims=True)
    upd_col = b_tot - b_col + i_col
    upd_max = jnp.max(b_tot - b_row + i_row, axis=1, keepdims=True)
    s_qk = _dot_nt(q, k.astype(BF16))
    yield

    c_aug, m = state
    inter = b_col + m
    m_t = jnp.maximum(inter, row_max)
    w_inter = jnp.exp(inter - m_t)
    w_intra = (jnp.exp(log_w - m_t) * s_qk).astype(BF16)
    m_new = jnp.maximum(b_tot + m, upd_max)
    w_s = jnp.exp(b_tot + m - m_new)
    w_u = jnp.exp(upd_col - m_new)
    state[1] = m_new
    upd = _dot_tn((w_u * k).astype(BF16), vaug)
    tot = w_inter * _dot(q, c_aug.astype(BF16)) + _dot(w_intra, vaug)
    state[0] = w_s * c_aug + upd
    h_ref[rows, :] = tot[:, :HEAD_W] / jnp.maximum(jnp.abs(tot[:, HEAD_W:]), jnp.exp(-m_t))


def _mlstm_kernel(fb_ref, q_ref, k_ref, v_ref, og_ref, gc_ref, gr_ref, nw_ref, y_ref,
                  hf_ref, hb_ref, cf_ref, cb_ref, mf_ref, mb_ref, *, n_ctx, n_tok):
    L, U = MLSTM_CHUNK, MLSTM_UNROLL
    nc, ncc = n_tok // L, n_ctx // L
    h = pl.program_id(1)
    fb_f = fb_ref[0, h]
    fb_b = fb_ref[1, h]
    for ref in (cf_ref, cb_ref, mf_ref, mb_ref):
        ref[...] = jnp.zeros_like(ref)

    def body(it, carry):
        sf = [cf_ref[...], mf_ref[...]]
        sb = [cb_ref[...], mb_ref[...]]
        gens = []
        for u in range(U):
            step = it * U + u
            gens.append(_mlstm_chunk(q_ref, k_ref, v_ref, gc_ref, gr_ref, fb_f, sf, hf_ref,
                                     pl.multiple_of(step * L, L), False))
            ck = jnp.where(step < ncc, ncc - 1 - step, nc - 1 - (step - ncc))
            gens.append(_mlstm_chunk(q_ref, k_ref, v_ref, gc_ref, gr_ref, fb_b, sb, hb_ref,
                                     pl.multiple_of(ck * L, L), True))
        _round_robin(gens)
        cf_ref[...], mf_ref[...] = sf
        cb_ref[...], mb_ref[...] = sb
        return carry

    lax.fori_loop(0, nc // U, body, 0)

    fr = 256
    def fin(i, carry):
        rows = pl.ds(pl.multiple_of(i * fr, fr), fr)
        o = jax.nn.sigmoid(og_ref[rows, :]) * (hf_ref[rows, :] + hb_ref[rows, :])
        y_ref[rows, :] = (_rms(o) * nw_ref[...]).astype(y_ref.dtype)
        return carry

    lax.fori_loop(0, n_tok // fr, fin, 0)


def _mlstm(p, gcol, grow, f_bias, norm_w, l, n_ctx):
    b, t, _ = p.shape
    base = 11 * HEADS

    def col(seg):
        return pl.BlockSpec((None, t, HEAD_W), lambda bi, h: (bi, 0, base + seg * HEADS + h))

    return pl.pallas_call(
        functools.partial(_mlstm_kernel, n_ctx=n_ctx, n_tok=t),
        grid=(b, HEADS),
        in_specs=[pl.BlockSpec(memory_space=pltpu.SMEM),
                  col(0), col(1), col(2), col(3),
                  pl.BlockSpec((None, None, t, 8), lambda bi, h: (bi, h, 0, 0)),
                  pl.BlockSpec((None, None, 8, t), lambda bi, h: (bi, h, 0, 0)),
                  pl.BlockSpec((None, 1, HEAD_W), lambda bi, h: (l, 0, h))],
        out_specs=pl.BlockSpec((None, t, HEAD_W), lambda bi, h: (bi, 0, h)),
        out_shape=jax.ShapeDtypeStruct((b, t, GROUP_W), BF16),
        scratch_shapes=[pltpu.VMEM((t, HEAD_W), F32), pltpu.VMEM((t, HEAD_W), F32),
                        pltpu.VMEM((HEAD_W, 2 * HEAD_W), F32), pltpu.VMEM((HEAD_W, 2 * HEAD_W), F32),
                        pltpu.VMEM((1, 1), F32), pltpu.VMEM((1, 1), F32)],
        compiler_params=_cparams("arbitrary", "arbitrary"),
    )(f_bias, p, p, p, p, gcol, grow, norm_w)


def _outproj_kernel(ya_ref, yb_ref, yc_ref, yd_ref, w_ref, x_ref, mx_ref, mc_ref, o_ref, *, n_ctx, tm):
    i = pl.program_id(1)
    acc = _dot(ya_ref[...], w_ref[0])
    acc += _dot(yb_ref[...], w_ref[1])
    acc += _dot(yc_ref[...], w_ref[2])
    acc += _dot(yd_ref[...], w_ref[3])
    rows = i * tm + lax.broadcasted_iota(jnp.int32, (tm, 1), 0)
    gate = jnp.where(rows < n_ctx, mc_ref[2:3, :], mx_ref[2:3, :])
    o_ref[...] = x_ref[...] + gate * acc


def _outproj(ys, w_out, xs, mods, l, n_ctx):
    b, t, d = xs.shape
    tm, tn = ROW_TILE, 1024
    nb = mods.shape[1] - 1
    yspec = pl.BlockSpec((None, tm, GROUP_W), lambda bi, i, j: (bi, i, 0))
    return pl.pallas_call(
        functools.partial(_outproj_kernel, n_ctx=n_ctx, tm=tm),
        grid=(b, t // tm, d // tn),
        in_specs=[yspec, yspec, yspec, yspec,
                  pl.BlockSpec((None, 4, GROUP_W, tn), lambda bi, i, j: (l, 0, 0, j)),
                  pl.BlockSpec((None, tm, tn), lambda bi, i, j: (bi, i, j)),
                  pl.BlockSpec((None, None, 6, tn), lambda bi, i, j: (l, bi, 0, j)),
                  pl.BlockSpec((None, None, 6, tn), lambda bi, i, j: (l, nb, 0, j))],
        out_specs=pl.BlockSpec((None, tm, tn), lambda bi, i, j: (bi, i, j)),
        out_shape=jax.ShapeDtypeStruct((b, t, d), F32),
        compiler_params=_cparams("arbitrary", "arbitrary", "arbitrary"),
    )(*ys, w_out, xs, mods, mods)


def _ffn_kernel(xm_ref, xp_ref, xn_ref, mx_ref, mc_ref, wa_ref, wg_ref, cwa_ref, cwg_ref,
                cba_ref, cbg_ref, wd_ref, o_ref, hs_ref, ua_ref, ug_ref, *, n_ctx, n_tok, tm):
    i = pl.program_id(1)
    f = pl.program_id(2)
    row0 = i * tm

    @pl.when(f == 0)
    def _():
        hs_ref[0:HALO, :] = _modulate(xp_ref[...], mx_ref, mc_ref, row0 - HALO, 3, n_ctx).astype(BF16)
        _modulate_into(hs_ref, HALO, xm_ref, mx_ref, mc_ref, row0, 3, n_ctx)
        hs_ref[HALO + tm:, :] = _modulate(xn_ref[...], mx_ref, mc_ref, row0 + tm, 3, n_ctx).astype(BF16)

    ua_ref[...] = _dot(hs_ref[...], wa_ref[...])
    ug_ref[...] = _dot(hs_ref[...], wg_ref[...])

    @pl.when(f == 0)
    def _():
        o_ref[...] = jnp.zeros_like(o_ref)

    for r in range(0, tm, FFN_SUB):
        rows = row0 + r + lax.broadcasted_iota(jnp.int32, (FFN_SUB, 1), 0)
        has_prev = (rows != 0) & (rows != n_ctx)
        has_next = (rows != n_ctx - 1) & (rows != n_tok - 1)

        def conv(u_ref, cw_ref, cb_ref):
            lo = HALO + r
            prev = jnp.where(has_prev, u_ref[lo - 1:lo - 1 + FFN_SUB, :], 0.0)
            nxt = jnp.where(has_next, u_ref[lo + 1:lo + 1 + FFN_SUB, :], 0.0)
            return (prev * cw_ref[0:1, :] + u_ref[lo:lo + FFN_SUB, :] * cw_ref[1:2, :]
                    + nxt * cw_ref[2:3, :] + cb_ref[...])

        act = (_silu(conv(ua_ref, cwa_ref, cba_ref)) * conv(ug_ref, cwg_ref, cbg_ref)).astype(BF16)
        o_ref[r:r + FFN_SUB, :] += _dot(act, wd_ref[...])

    @pl.when(f == pl.num_programs(2) - 1)
    def _():
        rows = row0 + lax.broadcasted_iota(jnp.int32, (tm, 1), 0)
        gate = jnp.where(rows < n_ctx, mc_ref[5:6, :], mx_ref[5:6, :])
        o_ref[...] = xm_ref[...] + gate * o_ref[...]


def _ffn(xs, mods, w_up, conv_w, conv_b, w_down, l, n_ctx):
    b, t, d = xs.shape
    tm, tf = ROW_TILE, FF_TILE
    d_ff = w_down.shape[1]
    nf = d_ff // tf
    nb = mods.shape[1] - 1
    hb = tm // HALO
    last = t // HALO - 1
    return pl.pallas_call(
        functools.partial(_ffn_kernel, n_ctx=n_ctx, n_tok=t, tm=tm),
        grid=(b, t // tm, nf),
        in_specs=[
            pl.BlockSpec((None, tm, d), lambda bi, i, f: (bi, i, 0)),
            pl.BlockSpec((None, HALO, d), lambda bi, i, f: (bi, jnp.maximum(i * hb - 1, 0), 0)),
            pl.BlockSpec((None, HALO, d), lambda bi, i, f: (bi, jnp.minimum((i + 1) * hb, last), 0)),
            pl.BlockSpec((None, None, 6, d), lambda bi, i, f: (l, bi, 0, 0)),
            pl.BlockSpec((None, None, 6, d), lambda bi, i, f: (l, nb, 0, 0)),
            pl.BlockSpec((None, d, tf), lambda bi, i, f: (l, 0, f)),
            pl.BlockSpec((None, d, tf), lambda bi, i, f: (l, 0, nf + f)),
            pl.BlockSpec((None, CONV_W, tf), lambda bi, i, f: (l, 0, f)),
            pl.BlockSpec((None, CONV_W, tf), lambda bi, i, f: (l, 0, nf + f)),
            pl.BlockSpec((None, 1, tf), lambda bi, i, f: (l, 0, f)),
            pl.BlockSpec((None, 1, tf), lambda bi, i, f: (l, 0, nf + f)),
            pl.BlockSpec((None, tf, d), lambda bi, i, f: (l, f, 0)),
        ],
        out_specs=pl.BlockSpec((None, tm, d), lambda bi, i, f: (bi, i, 0)),
        out_shape=jax.ShapeDtypeStruct((b, t, d), F32),
        scratch_shapes=[pltpu.VMEM((tm + 2 * HALO, d), BF16),
                        pltpu.VMEM((tm + 2 * HALO, tf), F32),
                        pltpu.VMEM((tm + 2 * HALO, tf), F32)],
        compiler_params=_cparams("arbitrary", "arbitrary", "arbitrary"),
    )(xs, xs, xs, mods, mods, w_up, w_up, conv_w, conv_w, conv_b, conv_b, w_down)


def _final_kernel(x_ref, w_ref, o_ref):
    o_ref[...] = _rms(x_ref[...]) * w_ref[...]


def _final_norm(xs, w, n_ctx):
    b, t, d = xs.shape
    tr = n_ctx
    off = n_ctx // tr
    return pl.pallas_call(
        _final_kernel,
        grid=(b, (t - n_ctx) // tr),
        in_specs=[pl.BlockSpec((None, tr, d), lambda bi, i: (bi, i + off, 0)),
                  pl.BlockSpec((1, d), lambda bi, i: (0, 0))],
        out_specs=pl.BlockSpec((None, tr, d), lambda bi, i: (bi, i, 0)),
        out_shape=jax.ShapeDtypeStruct((b, t - n_ctx, d), F32),
        compiler_params=_cparams("arbitrary", "arbitrary"),
    )(xs, w.reshape(1, d))


def _lambda_init(layer_idx):
    return 0.8 - 0.6 * math.exp(-0.3 * layer_idx)


def _rope_tables(n_ctx, n_lat):
    dh = HEAD_W // 2
    t = np.arange(n_lat)
    row = (t // GRID_W).astype(np.float32)
    colp = (t % GRID_W).astype(np.float32)
    half = dh // 2
    inv = (ROPE_THETA ** (-np.arange(0, half, 2, dtype=np.float32) / half)).astype(np.float32)
    ar = row[:, None] * inv
    ac = colp[:, None] * inv
    ang = jnp.asarray(np.concatenate([ar, ar, ac, ac], axis=-1))
    cos, sin = jnp.cos(ang), jnp.sin(ang)
    sign = np.where((np.arange(dh) % 32) < 16, -1.0, 1.0).astype(np.float32)
    cos = jnp.concatenate([jnp.ones((n_ctx, dh), F32), cos], axis=0)
    sin = jnp.concatenate([jnp.zeros((n_ctx, dh), F32), sin * sign], axis=0)
    return jnp.tile(cos, (1, 2)), jnp.tile(sin, (1, 2))


def _na_bias(na_rpb, kh):
    w = np.arange(GRID_W)
    col_start = np.clip(w - NA_KW // 2, 0, GRID_W - NA_KW)
    col_ok = (w[None, :] >= col_start[:, None]) & (w[None, :] < col_start[:, None] + NA_KW)
    coff = np.clip(w[None, :] - w[:, None], -(NA_KW - 1), NA_KW - 1) + (NA_KW - 1)
    onehot = jnp.asarray(coff[:, :, None] == np.arange(2 * NA_KW - 1), F32)
    toe = jnp.einsum('dhrc,qkc->dhrqk', na_rpb.astype(F32), onehot, precision=lax.Precision.HIGHEST)
    toe = jnp.where(col_ok, toe, NEG)
    per_id = [toe[:, :, NA_KH - 1 - i:NA_KH - 1 - i + kh] for i in range(kh)]
    bias = jnp.stack(per_id, axis=2).transpose(0, 1, 2, 4, 3, 5)
    return bias.reshape(na_rpb.shape[0], na_rpb.shape[1], kh, GRID_W, kh * GRID_W)


def kernel(x, c, ctx, c_ctx, w_ada, b_ada, w_in, hgrn_lb, hgrn_norm_w, diff_lam, diff_norm_w, na_rpb,
           mlstm_f_bias, mlstm_norm_w, w_out, w_up, conv_w, conv_b, w_down, final_norm_w):
    bsz, n_lat, d = x.shape
    n_ctx = ctx.shape[1]
    depth = w_in.shape[0]
    d_ff = w_down.shape[1]
    grid_rows = n_lat // GRID_W

    xs = jnp.concatenate([ctx, x], axis=1)
    n_mod = -(-(bsz + 1) // 8) * 8
    cc = jnp.zeros((n_mod, d), F32).at[:bsz].set(c).at[n_mod - 1].set(c_ctx)
    mods = _ada(cc, w_ada, b_ada).reshape(depth, n_mod, 6, d)

    w_in_b = w_in.astype(BF16)
    w_gate_b = jnp.pad(w_in[:, :, D_MAIN:], ((0, 0), (0, 0), (0, HEAD_W - N_GATES))).astype(BF16)
    w_out_b = w_out.astype(BF16).reshape(depth, 4, GROUP_W, d)
    w_up_b = w_up.astype(BF16)
    w_down_b = w_down.astype(BF16)
    conv_b3 = conv_b.reshape(depth, 1, 2 * d_ff)

    lb = jnp.cumsum(jax.nn.softmax(hgrn_lb.astype(F32), axis=0), axis=0)
    lb = lb - lb[:1]
    zero = jnp.zeros_like(lb[:, :1])
    gate_par = jnp.concatenate([jnp.log(lb[:, 0:1]), jnp.log1p(-lb[:, 0:1]), 1.0 - lb[:, 0:1],
                                jnp.log(lb[:, 1:2]), jnp.log1p(-lb[:, 1:2]), 1.0 - lb[:, 1:2],
                                zero, zero], axis=1)
    cos, sin_signed = _rope_tables(n_ctx, n_lat)
    lam = jnp.pad(diff_lam.astype(F32), ((0, 0), (0, 4), (0, HEAD_W - diff_lam.shape[2])))
    bias = _na_bias(na_rpb, min(NA_KH, grid_rows))
    f_bias = mlstm_f_bias.astype(F32)

    def nw(w):
        return w.astype(F32).reshape(depth, 1, GROUP_W)

    hgrn_nw, diff_nw, mlstm_nw = nw(hgrn_norm_w), nw(diff_norm_w), nw(mlstm_norm_w)

    for l in range(depth):
        p, gates = _inproj(xs, mods, w_in_b, w_gate_b, l, n_ctx)
        g4 = gates[:, :, :N_GATES].reshape(bsz, -1, 4, HEADS)
        gcol = jnp.pad(g4.transpose(0, 3, 1, 2), ((0, 0), (0, 0), (0, 0), (0, 4)))
        grow = jnp.pad(g4.transpose(0, 3, 2, 1), ((0, 0), (0, 0), (0, 4), (0, 0)))
        ya = _hgrn(p, gate_par, hgrn_nw, l, n_ctx)
        yb = _diff(p, cos, sin_signed, lam, diff_nw, l, n_ctx, _lambda_init(l))
        yc = _na(p, bias, l, n_ctx)
        yd = _mlstm(p, gcol, grow, f_bias[l], mlstm_nw, l, n_ctx)
        xs = _outproj((ya, yb, yc, yd), w_out_b, xs, mods, l, n_ctx)
        xs = _ffn(xs, mods, w_up_b, conv_w, conv_b3, w_down_b, l, n_ctx)
    return _final_norm(xs, final_norm_w, n_ctx)
```

```python
import functools
import math

import numpy as np
import jax
import jax.numpy as jnp
from jax import lax
from jax.experimental import pallas as pl
from jax.experimental.pallas import tpu as pltpu

F32 = jnp.float32
BF16 = jnp.bfloat16

HEADS = 4
HEAD_W = 128
GROUP_W = HEADS * HEAD_W
GRID_W = 64
NA_KH = 8
NA_KW = 16
ROPE_THETA = 10000.0
EPS = 1e-6
CONV_W = 3
D_MAIN = 15 * GROUP_W
N_GATES = 4 * HEADS

VMEM_LIMIT_BYTES = 56 * 1024 * 1024
NEG = -1e30
LOG2E = 1.4426950408889634

HGRN_CHUNK = 64
HGRN_SUB = 16
HGRN_UNROLL = 4
MLSTM_CHUNK = 128
MLSTM_UNROLL = 3
NA_UNROLL = 16
ROW_TILE = 768
INPROJ_TILE = 1280
FF_TILE = 512
HALO = 16
FFN_ROW_TILE = 768
FFN_SUB = 256
MOD_SUB = 256


def _unroll(n, target):
    return max(u for u in range(1, target + 1) if n % u == 0)


def _cparams(*sem):
    return pltpu.CompilerParams(dimension_semantics=sem, vmem_limit_bytes=VMEM_LIMIT_BYTES)


def _dot(a, b):
    return jnp.dot(a, b, preferred_element_type=F32)


def _dot_nt(a, b):
    return lax.dot_general(a, b, (((1,), (1,)), ((), ())), preferred_element_type=F32)


def _dot_tn(a, b):
    return lax.dot_general(a, b, (((0,), (0,)), ((), ())), preferred_element_type=F32)


def _silu(t):
    return t * jax.nn.sigmoid(t)


def _log_sigmoid(z):
    return jnp.minimum(z, 0.0) - jnp.log1p(jnp.exp(-jnp.abs(z)))


def _rms(t):
    return t * lax.rsqrt(jnp.mean(t * t, axis=-1, keepdims=True) + EPS)


def _split3(t):
    hi = t.astype(BF16)
    r1 = t - hi.astype(F32)
    mid = r1.astype(BF16)
    lo = (r1 - mid.astype(F32)).astype(BF16)
    return hi, mid, lo


def _modulate(x, mx_ref, mc_ref, row0, k, n_ctx):
    rows = row0 + lax.broadcasted_iota(jnp.int32, (x.shape[0], 1), 0)
    is_ctx = rows < n_ctx
    shift = jnp.where(is_ctx, mc_ref[k:k + 1, :], mx_ref[k:k + 1, :])
    scale = jnp.where(is_ctx, mc_ref[k + 1:k + 2, :], mx_ref[k + 1:k + 2, :])
    return _rms(x) * (1.0 + scale) + shift


def _modulate_into(dst_ref, dst_off, x_ref, mx_ref, mc_ref, row0, k, n_ctx):
    n = x_ref.shape[0]
    sub = math.gcd(n, MOD_SUB)
    for r in range(0, n, sub):
        h = _modulate(x_ref[r:r + sub, :], mx_ref, mc_ref, row0 + r, k, n_ctx)
        dst_ref[dst_off + r:dst_off + r + sub, :] = h.astype(dst_ref.dtype)


def _ada_kernel(c_ref, w_ref, b_ref, o_ref):
    s = _silu(c_ref[...]).astype(BF16)
    o_ref[...] = _dot(s, w_ref[...].astype(BF16)) + b_ref[...]


def _ada(cc, w_ada, b_ada):
    depth, d, n = w_ada.shape
    tn = 1024
    return pl.pallas_call(
        _ada_kernel,
        grid=(depth, n // tn),
        in_specs=[
            pl.BlockSpec((8, d), lambda l, j: (0, 0)),
            pl.BlockSpec((None, d, tn), lambda l, j: (l, 0, j)),
            pl.BlockSpec((None, 1, tn), lambda l, j: (l, 0, j)),
        ],
        out_specs=pl.BlockSpec((None, 8, tn), lambda l, j: (l, 0, j)),
        out_shape=jax.ShapeDtypeStruct((depth, 8, n), F32),
        compiler_params=_cparams("arbitrary", "arbitrary"),
    )(cc, w_ada, b_ada.reshape(depth, 1, n))


def _inproj_kernel(x_ref, mx_ref, mc_ref, w_ref, wg_ref, p_ref, gate_ref, h_ref, *, n_ctx, tm):
    i = pl.program_id(1)
    j = pl.program_id(2)

    @pl.when(j == 0)
    def _():
        _modulate_into(h_ref, 0, x_ref, mx_ref, mc_ref, i * tm, 0, n_ctx)
        gate_ref[...] = _dot(h_ref[...], wg_ref[...])

    p_ref[...] = _dot(h_ref[...], w_ref[...])


def _inproj(xs, mods, w_in, w_gate, l, n_ctx):
    b, t, d = xs.shape
    tm, tn = ROW_TILE, INPROJ_TILE
    nb = mods.shape[1] - 1
    return pl.pallas_call(
        functools.partial(_inproj_kernel, n_ctx=n_ctx, tm=tm),
        grid=(b, t // tm, D_MAIN // tn),
        in_specs=[
            pl.BlockSpec((None, tm, d), lambda bi, i, j: (bi, i, 0)),
            pl.BlockSpec((None, None, 6, d), lambda bi, i, j: (l, bi, 0, 0)),
            pl.BlockSpec((None, None, 6, d), lambda bi, i, j: (l, nb, 0, 0)),
            pl.BlockSpec((None, d, tn), lambda bi, i, j: (l, 0, j)),
            pl.BlockSpec((None, d, HEAD_W), lambda bi, i, j: (l, 0, 0)),
        ],
        out_specs=[
            pl.BlockSpec((None, tm, tn), lambda bi, i, j: (bi, i, j)),
            pl.BlockSpec((None, tm, HEAD_W), lambda bi, i, j: (bi, i, 0)),
        ],
        out_shape=[
            jax.ShapeDtypeStruct((b, t, D_MAIN), F32),
            jax.ShapeDtypeStruct((b, t, HEAD_W), F32),
        ],
        scratch_shapes=[pltpu.VMEM((tm, d), BF16)],
        compiler_params=_cparams("arbitrary", "arbitrary", "arbitrary"),
    )(xs, mods, mods, w_in, w_gate)


def _round_robin(gens):
    live = list(gens)
    while live:
        nxt = []
        for g in live:
            try:
                next(g)
                nxt.append(g)
            except StopIteration:
                pass
        live = nxt


def _hgrn_chunk(q_ref, v_ref, z_ref, gp_ref, state, o_ref, r0, rev):
    L, SC = HGRN_CHUNK, HGRN_SUB
    nsub = L // SC
    rows = pl.ds(r0, L)
    q = _silu(q_ref[rows, :]) * (HEAD_W ** -0.5)
    z = z_ref[rows, :]
    v = v_ref[rows, :].astype(BF16)
    e = jnp.exp(-jnp.abs(z))
    inv = 1.0 / (1.0 + e)
    a = gp_ref[1:2, :] + (jnp.minimum(z, 0.0) - jnp.log(1.0 + e))
    ll = gp_ref[0:1, :]
    lf = jnp.maximum(ll, a) + jnp.log(1.0 + jnp.exp(-jnp.abs(ll - a)))
    k = gp_ref[2:3, :] * (jnp.where(z >= 0.0, e, 1.0) * inv)

    t_i = lax.broadcasted_iota(jnp.int32, (L, L), 0)
    s_i = lax.broadcasted_iota(jnp.int32, (L, L), 1)
    t_blk = t_i & ~(SC - 1)
    if not rev:
        local = (s_i <= t_i) & (s_i >= t_blk)
        causal = s_i <= t_i
    else:
        local = (s_i >= t_i) & (s_i < t_blk + SC)
        causal = s_i >= t_i
    sel = jnp.where(local, 1.0, 0.0).astype(BF16)
    hi, mid, lo = _split3(lf)
    bl = _dot(sel, hi) + _dot(sel, mid) + _dot(sel, lo)
    yield

    def blk(x, i):
        return x[i * SC:(i + 1) * SC]

    order = list(range(nsub)) if not rev else list(range(nsub - 1, -1, -1))
    r = [None] * nsub
    acc = jnp.zeros((1, HEAD_W), F32)
    for i in order:
        r[i] = acc
        acc = acc + (bl[(i + 1) * SC - 1:(i + 1) * SC] if not rev else bl[i * SC:i * SC + 1])
    b_end = acc

    qt = q * jnp.exp(bl)
    kinv = k * jnp.exp(-bl)
    kd = jnp.concatenate([blk(kinv, j) * jnp.exp(b_end - r[j]) for j in range(nsub)], axis=0).astype(BF16)
    upd = _dot_tn(v, kd)
    zero = jnp.zeros((SC, HEAD_W), F32)
    parts = []
    for i in range(nsub):
        seen = [j for j in range(nsub) if (j <= i if not rev else j >= i)]
        kk = jnp.concatenate(
            [(blk(kinv, j) if j == i else blk(kinv, j) * jnp.exp(r[i] - r[j])) if j in seen else zero
             for j in range(nsub)], axis=0).astype(BF16)
        parts.append(_dot_nt(blk(qt, i).astype(BF16), kk))
    qb = jnp.concatenate([blk(qt, i) * jnp.exp(r[i]) for i in range(nsub)], axis=0).astype(BF16)
    yield

    st = state[0]
    amat = jnp.where(causal, jnp.concatenate(parts, axis=0), 0.0)
    o_ref[rows, :] = _dot_nt(qb, st.astype(BF16)) + _dot(amat.astype(BF16), v)
    state[0] = st * jnp.exp(b_end) + upd


def _hgrn_kernel(q_ref, v_ref, zf_ref, zb_ref, g_ref, gp_ref, nw_ref, y_ref,
                 of_ref, ob_ref, sf_ref, sb_ref, *, n_ctx, n_tok):
    L = HGRN_CHUNK
    nc, ncc = n_tok // L, n_ctx // L
    U = _unroll(nc, HGRN_UNROLL)
    sf_ref[...] = jnp.zeros_like(sf_ref)
    sb_ref[...] = jnp.zeros_like(sb_ref)

    def body(it, carry):
        sf = [sf_ref[...]]
        sb = [sb_ref[...]]
        gens = []
        for u in range(U):
            step = it * U + u
            gens.append(_hgrn_chunk(q_ref, v_ref, zf_ref, gp_ref.at[0:3], sf, of_ref,
                                    pl.multiple_of(step * L, L), False))
            cb = jnp.where(step < ncc, ncc - 1 - step, nc - 1 - (step - ncc))
            gens.append(_hgrn_chunk(q_ref, v_ref, zb_ref, gp_ref.at[3:6], sb, ob_ref,
                                    pl.multiple_of(cb * L, L), True))
        _round_robin(gens)
        sf_ref[...] = sf[0]
        sb_ref[...] = sb[0]
        return carry

    lax.fori_loop(0, nc // U, body, 0)

    fr = 256
    def fin(i, carry):
        rows = pl.ds(pl.multiple_of(i * fr, fr), fr)
        o = of_ref[rows, :] + ob_ref[rows, :]
        y_ref[rows, :] = (_rms(o) * nw_ref[...] * _silu(g_ref[rows, :])).astype(y_ref.dtype)
        return carry

    lax.fori_loop(0, n_tok // fr, fin, 0)


def _hgrn(p, gate_par, norm_w, l, n_ctx):
    b, t, _ = p.shape
    nh = GROUP_W // HEAD_W

    def col(seg):
        return pl.BlockSpec((None, t, HEAD_W), lambda bi, h: (bi, 0, seg * nh + h))

    return pl.pallas_call(
        functools.partial(_hgrn_kernel, n_ctx=n_ctx, n_tok=t),
        grid=(b, HEADS),
        in_specs=[col(0), col(1), col(2), col(3), col(4),
                  pl.BlockSpec((None, 8, HEAD_W), lambda bi, h: (l, 0, h)),
                  pl.BlockSpec((None, 1, HEAD_W), lambda bi, h: (l, 0, h))],
        out_specs=pl.BlockSpec((None, t, HEAD_W), lambda bi, h: (bi, 0, h)),
        out_shape=jax.ShapeDtypeStruct((b, t, GROUP_W), BF16),
        scratch_shapes=[pltpu.VMEM((t, HEAD_W), F32), pltpu.VMEM((t, HEAD_W), F32),
                        pltpu.VMEM((HEAD_W, HEAD_W), F32), pltpu.VMEM((HEAD_W, HEAD_W), F32)],
        compiler_params=_cparams("arbitrary", "arbitrary"),
    )(p, p, p, p, p, gate_par, norm_w)


def _rope(t, cos, sin_signed):
    lane = lax.broadcasted_iota(jnp.int32, t.shape, 1)
    first = (lane & 31) < 16
    rot = jnp.where(first, pltpu.roll(t, HEAD_W - 16, 1), pltpu.roll(t, 16, 1))
    return t * cos + rot * sin_signed


def _softmax_rows(s):
    e = jnp.exp(s - jnp.max(s, axis=-1, keepdims=True))
    return e * (1.0 / jnp.sum(e, axis=-1, keepdims=True))


def _diff_kernel(q_ref, k_ref, v_ref, cos_ref, sin_ref, lam_ref, nw_ref, y_ref, kb_ref, vb_ref,
                 *, n_ctx, tq, lam_init):
    qi = pl.program_id(2)

    @pl.when(qi == 0)
    def _():
        kb_ref[...] = _rope(k_ref[...], cos_ref[...], sin_ref[...]).astype(BF16)
        v = v_ref[...]
        vb_ref[...] = jnp.concatenate([v, jnp.ones_like(v)], axis=1).astype(BF16)

    lam = lam_ref[...]
    lam_full = (jnp.exp(jnp.sum(lam[0:1] * lam[1:2], axis=-1, keepdims=True))
                - jnp.exp(jnp.sum(lam[2:3] * lam[3:4], axis=-1, keepdims=True)) + lam_init)
    rows = pl.ds(pl.multiple_of(qi * tq, tq), tq)
    q = _rope(q_ref[...], cos_ref[rows, :], sin_ref[rows, :]) * ((HEAD_W // 2) ** -0.5 * LOG2E)
    lane = lax.broadcasted_iota(jnp.int32, q.shape, 1)
    q0 = jnp.where(lane < HEAD_W // 2, q, 0.0).astype(BF16)
    q1 = jnp.where(lane >= HEAD_W // 2, q, 0.0).astype(BF16)

    def attend(kb, vb):
        s0 = _dot_nt(q0, kb)
        s1 = _dot_nt(q1, kb)
        e0 = jnp.exp2(s0 - jnp.max(s0, axis=-1, keepdims=True)).astype(BF16)
        e1 = jnp.exp2(s1 - jnp.max(s1, axis=-1, keepdims=True)).astype(BF16)
        t0 = _dot(e0, vb)
        t1 = _dot(e1, vb)
        o = t0[:, :HEAD_W] / t0[:, HEAD_W:] - lam_full * (t1[:, :HEAD_W] / t1[:, HEAD_W:])
        y_ref[...] = (_rms(o) * nw_ref[...] * (1.0 - lam_init)).astype(y_ref.dtype)

    @pl.when(qi == 0)
    def _():
        attend(kb_ref[0:n_ctx, :], vb_ref[0:n_ctx, :])

    @pl.when(qi > 0)
    def _():
        attend(kb_ref[...], vb_ref[...])


def _diff(p, cos, sin_signed, lam, norm_w, l, n_ctx, lam_init):
    b, t, _ = p.shape
    tq = n_ctx
    base = 5 * HEADS

    def col(seg, rows, rmap):
        return pl.BlockSpec((None, rows, HEAD_W), lambda bi, h, qi: (bi, rmap(qi), base + seg * HEADS + h))

    return pl.pallas_call(
        functools.partial(_diff_kernel, n_ctx=n_ctx, tq=tq, lam_init=lam_init),
        grid=(b, HEADS, t // tq),
        in_specs=[col(0, tq, lambda qi: qi), col(1, t, lambda qi: 0), col(2, t, lambda qi: 0),
                  pl.BlockSpec((t, HEAD_W), lambda bi, h, qi: (0, 0)),
                  pl.BlockSpec((t, HEAD_W), lambda bi, h, qi: (0, 0)),
                  pl.BlockSpec((None, 8, HEAD_W), lambda bi, h, qi: (l, 0, 0)),
                  pl.BlockSpec((None, 1, HEAD_W), lambda bi, h, qi: (l, 0, h))],
        out_specs=pl.BlockSpec((None, tq, HEAD_W), lambda bi, h, qi: (bi, qi, h)),
        out_shape=jax.ShapeDtypeStruct((b, t, GROUP_W), BF16),
        scratch_shapes=[pltpu.VMEM((t, HEAD_W), BF16), pltpu.VMEM((t, 2 * HEAD_W), BF16)],
        compiler_params=_cparams("arbitrary", "arbitrary", "arbitrary"),
    )(p, p, p, cos, sin_signed, lam, norm_w)


def _na_kernel(q_ref, k_ref, v_ref, bias_ref, y_ref, kb_ref, vb_ref, *, n_ctx, grid_rows, kh):
    scale = HEAD_W ** -0.5
    kb_ref[...] = k_ref[...].astype(BF16)
    vb_ref[...] = v_ref[...].astype(BF16)
    kc = kb_ref[0:n_ctx, :]
    vc = vb_ref[0:n_ctx, :]

    qc = (q_ref[0:n_ctx, :] * scale).astype(BF16)
    pc = _softmax_rows(_dot_nt(qc, kc))
    y_ref[0:n_ctx, :] = _dot(pc.astype(BF16), vc).astype(y_ref.dtype)

    win = kh * GRID_W

    def one_row(r, out):
        rows = pl.ds(pl.multiple_of(n_ctx + r * GRID_W, GRID_W), GRID_W)
        q = (q_ref[rows, :] * scale).astype(BF16)
        start = jnp.clip(r - kh // 2, 0, grid_rows - kh)
        wrows = pl.ds(pl.multiple_of(n_ctx + start * GRID_W, GRID_W), win)
        sb = _dot_nt(q, kb_ref[wrows, :]) + bias_ref[r - start]
        sc = _dot_nt(q, kc)
        yield
        m = jnp.maximum(jnp.max(sb, axis=-1, keepdims=True), jnp.max(sc, axis=-1, keepdims=True))
        eb = jnp.exp(sb - m)
        ec = jnp.exp(sc - m)
        inv = 1.0 / (jnp.sum(eb, axis=-1, keepdims=True) + jnp.sum(ec, axis=-1, keepdims=True))
        pb = (eb * inv).astype(BF16)
        pc_ = (ec * inv).astype(BF16)
        yield
        out.append(_dot(pb, vb_ref[wrows, :]) + _dot(pc_, vc))

    nu = _unroll(grid_rows, NA_UNROLL)

    def body(it, carry):
        outs = [[] for _ in range(nu)]
        _round_robin([one_row(it * nu + u, outs[u]) for u in range(nu)])
        o = jnp.concatenate([x[0] for x in outs], axis=0)
        rows = pl.ds(pl.multiple_of(n_ctx + it * (nu * GRID_W), nu * GRID_W), nu * GRID_W)
        y_ref[rows, :] = o.astype(y_ref.dtype)
        return carry

    lax.fori_loop(0, grid_rows // nu, body, 0)


def _na(p, bias, l, n_ctx):
    b, t, _ = p.shape
    grid_rows = (t - n_ctx) // GRID_W
    kh = min(NA_KH, grid_rows)
    base = 8 * HEADS

    def col(seg):
        return pl.BlockSpec((None, t, HEAD_W), lambda bi, h: (bi, 0, base + seg * HEADS + h))

    return pl.pallas_call(
        functools.partial(_na_kernel, n_ctx=n_ctx, grid_rows=grid_rows, kh=kh),
        grid=(b, HEADS),
        in_specs=[col(0), col(1), col(2),
                  pl.BlockSpec((None, None, kh, GRID_W, kh * GRID_W), lambda bi, h: (l, h, 0, 0, 0))],
        out_specs=pl.BlockSpec((None, t, HEAD_W), lambda bi, h: (bi, 0, h)),
        out_shape=jax.ShapeDtypeStruct((b, t, GROUP_W), BF16),
        scratch_shapes=[pltpu.VMEM((t, HEAD_W), BF16), pltpu.VMEM((t, HEAD_W), BF16)],
        compiler_params=_cparams("arbitrary", "arbitrary"),
    )(p, p, p, bias)


def _mlstm_chunk(q_ref, k_ref, v_ref, gr_ref, fb, state, h_ref, r0, rev):
    L = MLSTM_CHUNK
    rows = pl.ds(r0, L)
    gi, gf = (2, 3) if rev else (0, 1)
    q = (q_ref[rows, :] * (HEAD_W ** -0.5)).astype(BF16)
    k = k_ref[rows, :]
    v = v_ref[rows, :]
    vaug = jnp.concatenate([v, jnp.ones_like(v)], axis=1).astype(BF16)
    grow = gr_ref[:, rows]
    i_row = grow[gi:gi + 1, :]
    f_rows = _log_sigmoid(grow + fb)

    t_i = lax.broadcasted_iota(jnp.int32, (L, L), 0)
    s_i = lax.broadcasted_iota(jnp.int32, (L, L), 1)
    causal = (s_i <= t_i) if not rev else (s_i >= t_i)
    b_col = jnp.sum(jnp.where(causal, f_rows[gf:gf + 1, :], 0.0), axis=1, keepdims=True)
    tri_t = jnp.where((t_i <= s_i) if not rev else (t_i >= s_i), 1.0, 0.0).astype(BF16)
    fr3 = _split3(f_rows)
    b_row = (_dot(fr3[0], tri_t) + _dot(fr3[1], tri_t) + _dot(fr3[2], tri_t))[gf:gf + 1, :]
    b_tot = b_row[:, L - 1:L] if not rev else b_row[:, 0:1]
    log_w = jnp.where(causal, b_col - b_row + i_row, NEG)
    row_max = jnp.max(log_w, axis=1, keepdims=True)
    upd_row = b_tot - b_row + i_row
    upd_max = jnp.max(upd_row, axis=1, keepdims=True)
    s_qk = _dot_nt(q, k.astype(BF16))
    k_t = k.T
    yield

    m = state[1]
    inter = b_col + m
    m_t = jnp.maximum(inter, row_max)
    w_inter = jnp.exp(inter - m_t)
    w_intra = (jnp.exp(log_w - m_t) * s_qk).astype(BF16)
    m_new = jnp.maximum(b_tot + m, upd_max)
    w_s = jnp.exp(b_tot + m - m_new)
    state[1] = m_new
    upd = _dot((k_t * jnp.exp(upd_row - m_new)).astype(BF16), vaug)
    intra = _dot(w_intra, vaug)
    yield

    c_aug = state[0]
    tot = w_inter * _dot(q, c_aug.astype(BF16)) + intra
    state[0] = w_s * c_aug + upd
    h_ref[rows, :] = tot[:, :HEAD_W] / jnp.maximum(jnp.abs(tot[:, HEAD_W:]), jnp.exp(-m_t))


def _mlstm_kernel(fb_ref, q_ref, k_ref, v_ref, og_ref, gr_ref, nw_ref, y_ref,
                  hf_ref, hb_ref, cf_ref, cb_ref, mf_ref, mb_ref, *, n_ctx, n_tok):
    L = MLSTM_CHUNK
    nc, ncc = n_tok // L, n_ctx // L
    U = _unroll(nc, MLSTM_UNROLL)
    h = pl.program_id(1)
    fb_f = fb_ref[0, h]
    fb_b = fb_ref[1, h]
    for ref in (cf_ref, cb_ref, mf_ref, mb_ref):
        ref[...] = jnp.zeros_like(ref)

    def body(it, carry):
        sf = [cf_ref[...], mf_ref[...]]
        sb = [cb_ref[...], mb_ref[...]]
        gens = []
        for u in range(U):
            step = it * U + u
            gens.append(_mlstm_chunk(q_ref, k_ref, v_ref, gr_ref, fb_f, sf, hf_ref,
                                     pl.multiple_of(step * L, L), False))
            ck = jnp.where(step < ncc, ncc - 1 - step, nc - 1 - (step - ncc))
            gens.append(_mlstm_chunk(q_ref, k_ref, v_ref, gr_ref, fb_b, sb, hb_ref,
                                     pl.multiple_of(ck * L, L), True))
        _round_robin(gens)
        cf_ref[...], mf_ref[...] = sf
        cb_ref[...], mb_ref[...] = sb
        return carry

    lax.fori_loop(0, nc // U, body, 0)

    fr = 256
    def fin(i, carry):
        rows = pl.ds(pl.multiple_of(i * fr, fr), fr)
        o = jax.nn.sigmoid(og_ref[rows, :]) * (hf_ref[rows, :] + hb_ref[rows, :])
        y_ref[rows, :] = (_rms(o) * nw_ref[...]).astype(y_ref.dtype)
        return carry

    lax.fori_loop(0, n_tok // fr, fin, 0)


def _mlstm(p, grow, f_bias, norm_w, l, n_ctx):
    b, t, _ = p.shape
    base = 11 * HEADS

    def col(seg):
        return pl.BlockSpec((None, t, HEAD_W), lambda bi, h: (bi, 0, base + seg * HEADS + h))

    return pl.pallas_call(
        functools.partial(_mlstm_kernel, n_ctx=n_ctx, n_tok=t),
        grid=(b, HEADS),
        in_specs=[pl.BlockSpec(memory_space=pltpu.SMEM),
                  col(0), col(1), col(2), col(3),
                  pl.BlockSpec((None, None, 8, t), lambda bi, h: (bi, h, 0, 0)),
                  pl.BlockSpec((None, 1, HEAD_W), lambda bi, h: (l, 0, h))],
        out_specs=pl.BlockSpec((None, t, HEAD_W), lambda bi, h: (bi, 0, h)),
        out_shape=jax.ShapeDtypeStruct((b, t, GROUP_W), BF16),
        scratch_shapes=[pltpu.VMEM((t, HEAD_W), F32), pltpu.VMEM((t, HEAD_W), F32),
                        pltpu.VMEM((HEAD_W, 2 * HEAD_W), F32), pltpu.VMEM((HEAD_W, 2 * HEAD_W), F32),
                        pltpu.VMEM((1, 1), F32), pltpu.VMEM((1, 1), F32)],
        compiler_params=_cparams("arbitrary", "arbitrary"),
    )(f_bias, p, p, p, p, grow, norm_w)


def _outproj_kernel(ya_ref, yb_ref, yc_ref, yd_ref, w_ref, x_ref, mx_ref, mc_ref, o_ref, *, n_ctx, tm):
    i = pl.program_id(1)
    acc = _dot(ya_ref[...], w_ref[0])
    acc += _dot(yb_ref[...], w_ref[1])
    acc += _dot(yc_ref[...], w_ref[2])
    acc += _dot(yd_ref[...], w_ref[3])
    rows = i * tm + lax.broadcasted_iota(jnp.int32, (tm, 1), 0)
    gate = jnp.where(rows < n_ctx, mc_ref[2:3, :], mx_ref[2:3, :])
    o_ref[...] = x_ref[...] + gate * acc


def _outproj(ys, w_out, xs, mods, l, n_ctx):
    b, t, d = xs.shape
    tm, tn = ROW_TILE, 1024
    nb = mods.shape[1] - 1
    yspec = pl.BlockSpec((None, tm, GROUP_W), lambda bi, i, j: (bi, i, 0))
    return pl.pallas_call(
        functools.partial(_outproj_kernel, n_ctx=n_ctx, tm=tm),
        grid=(b, t // tm, d // tn),
        in_specs=[yspec, yspec, yspec, yspec,
                  pl.BlockSpec((None, 4, GROUP_W, tn), lambda bi, i, j: (l, 0, 0, j)),
                  pl.BlockSpec((None, tm, tn), lambda bi, i, j: (bi, i, j)),
                  pl.BlockSpec((None, None, 6, tn), lambda bi, i, j: (l, bi, 0, j)),
                  pl.BlockSpec((None, None, 6, tn), lambda bi, i, j: (l, nb, 0, j))],
        out_specs=pl.BlockSpec((None, tm, tn), lambda bi, i, j: (bi, i, j)),
        out_shape=jax.ShapeDtypeStruct((b, t, d), F32),
        compiler_params=_cparams("arbitrary", "arbitrary", "arbitrary"),
    )(*ys, w_out, xs, mods, mods)


def _ffn_kernel(xm_ref, xp_ref, xn_ref, mx_ref, mc_ref, wa_ref, wg_ref, cwa_ref, cwg_ref,
                cba_ref, cbg_ref, wd_ref, o_ref, hs_ref, ua_ref, ug_ref, *, n_ctx, n_tok, tm):
    i = pl.program_id(1)
    f = pl.program_id(2)
    nf = pl.num_programs(2) - 1
    row0 = i * tm
    cur = f % 2
    prev = 1 - cur

    def up_project():
        ua_ref[cur] = _dot(hs_ref[...], wa_ref[...])
        ug_ref[cur] = _dot(hs_ref[...], wg_ref[...])

    def finish():
        ua = ua_ref.at[prev]
        ug = ug_ref.at[prev]
        for r in range(0, tm, FFN_SUB):
            rows = row0 + r + lax.broadcasted_iota(jnp.int32, (FFN_SUB, 1), 0)
            has_prev = (rows != 0) & (rows != n_ctx)
            has_next = (rows != n_ctx - 1) & (rows != n_tok - 1)

            def conv(u_ref, cw_ref, cb_ref):
                lo = HALO + r
                before = jnp.where(has_prev, u_ref[lo - 1:lo - 1 + FFN_SUB, :], 0.0)
                after = jnp.where(has_next, u_ref[lo + 1:lo + 1 + FFN_SUB, :], 0.0)
                return (before * cw_ref[0:1, :] + u_ref[lo:lo + FFN_SUB, :] * cw_ref[1:2, :]
                        + after * cw_ref[2:3, :] + cb_ref[...])

            act = (_silu(conv(ua, cwa_ref, cba_ref)) * conv(ug, cwg_ref, cbg_ref)).astype(BF16)
            o_ref[r:r + FFN_SUB, :] += _dot(act, wd_ref[...])

    @pl.when(f == 0)
    def _():
        hs_ref[0:HALO, :] = _modulate(xp_ref[...], mx_ref, mc_ref, row0 - HALO, 3, n_ctx).astype(BF16)
        _modulate_into(hs_ref, HALO, xm_ref, mx_ref, mc_ref, row0, 3, n_ctx)
        hs_ref[HALO + tm:, :] = _modulate(xn_ref[...], mx_ref, mc_ref, row0 + tm, 3, n_ctx).astype(BF16)
        o_ref[...] = jnp.zeros_like(o_ref)
        up_project()

    @pl.when((f > 0) & (f < nf))
    def _():
        up_project()
        finish()

    @pl.when(f == nf)
    def _():
        finish()
        rows = row0 + lax.broadcasted_iota(jnp.int32, (tm, 1), 0)
        gate = jnp.where(rows < n_ctx, mc_ref[5:6, :], mx_ref[5:6, :])
        o_ref[...] = xm_ref[...] + gate * o_ref[...]


def _ffn(xs, mods, w_up, conv_w, conv_b, w_down, l, n_ctx):
    b, t, d = xs.shape
    tm, tf = FFN_ROW_TILE, FF_TILE
    d_ff = w_down.shape[1]
    nf = d_ff // tf
    nb = mods.shape[1] - 1
    hb = tm // HALO
    last = t // HALO - 1

    def up(f):
        return jnp.minimum(f, nf - 1)

    def fin(f):
        return jnp.maximum(f - 1, 0)

    return pl.pallas_call(
        functools.partial(_ffn_kernel, n_ctx=n_ctx, n_tok=t, tm=tm),
        grid=(b, t // tm, nf + 1),
        in_specs=[
            pl.BlockSpec((None, tm, d), lambda bi, i, f: (bi, i, 0)),
            pl.BlockSpec((None, HALO, d), lambda bi, i, f: (bi, jnp.maximum(i * hb - 1, 0), 0)),
            pl.BlockSpec((None, HALO, d), lambda bi, i, f: (bi, jnp.minimum((i + 1) * hb, last), 0)),
            pl.BlockSpec((None, None, 6, d), lambda bi, i, f: (l, bi, 0, 0)),
            pl.BlockSpec((None, None, 6, d), lambda bi, i, f: (l, nb, 0, 0)),
            pl.BlockSpec((None, d, tf), lambda bi, i, f: (l, 0, up(f))),
            pl.BlockSpec((None, d, tf), lambda bi, i, f: (l, 0, nf + up(f))),
            pl.BlockSpec((None, CONV_W, tf), lambda bi, i, f: (l, 0, fin(f))),
            pl.BlockSpec((None, CONV_W, tf), lambda bi, i, f: (l, 0, nf + fin(f))),
            pl.BlockSpec((None, 1, tf), lambda bi, i, f: (l, 0, fin(f))),
            pl.BlockSpec((None, 1, tf), lambda bi, i, f: (l, 0, nf + fin(f))),
            pl.BlockSpec((None, tf, d), lambda bi, i, f: (l, fin(f), 0)),
        ],
        out_specs=pl.BlockSpec((None, tm, d), lambda bi, i, f: (bi, i, 0)),
        out_shape=jax.ShapeDtypeStruct((b, t, d), F32),
        scratch_shapes=[pltpu.VMEM((tm + 2 * HALO, d), BF16),
                        pltpu.VMEM((2, tm + 2 * HALO, tf), F32),
                        pltpu.VMEM((2, tm + 2 * HALO, tf), F32)],
        compiler_params=_cparams("arbitrary", "arbitrary", "arbitrary"),
    )(xs, xs, xs, mods, mods, w_up, w_up, conv_w, conv_w, conv_b, conv_b, w_down)


def _final_kernel(x_ref, w_ref, o_ref):
    o_ref[...] = _rms(x_ref[...]) * w_ref[...]


def _final_norm(xs, w, n_ctx):
    b, t, d = xs.shape
    tr = n_ctx
    off = n_ctx // tr
    return pl.pallas_call(
        _final_kernel,
        grid=(b, (t - n_ctx) // tr),
        in_specs=[pl.BlockSpec((None, tr, d), lambda bi, i: (bi, i + off, 0)),
                  pl.BlockSpec((1, d), lambda bi, i: (0, 0))],
        out_specs=pl.BlockSpec((None, tr, d), lambda bi, i: (bi, i, 0)),
        out_shape=jax.ShapeDtypeStruct((b, t - n_ctx, d), F32),
        compiler_params=_cparams("arbitrary", "arbitrary"),
    )(xs, w.reshape(1, d))


def _lambda_init(layer_idx):
    return 0.8 - 0.6 * math.exp(-0.3 * layer_idx)


def _rope_tables(n_ctx, n_lat):
    dh = HEAD_W // 2
    t = np.arange(n_lat)
    row = (t // GRID_W).astype(np.float32)
    colp = (t % GRID_W).astype(np.float32)
    half = dh // 2
    inv = (ROPE_THETA ** (-np.arange(0, half, 2, dtype=np.float32) / half)).astype(np.float32)
    ar = row[:, None] * inv
    ac = colp[:, None] * inv
    ang = jnp.asarray(np.concatenate([ar, ar, ac, ac], axis=-1))
    cos, sin = jnp.cos(ang), jnp.sin(ang)
    sign = np.where((np.arange(dh) % 32) < 16, -1.0, 1.0).astype(np.float32)
    cos = jnp.concatenate([jnp.ones((n_ctx, dh), F32), cos], axis=0)
    sin = jnp.concatenate([jnp.zeros((n_ctx, dh), F32), sin * sign], axis=0)
    return jnp.tile(cos, (1, 2)), jnp.tile(sin, (1, 2))


def _na_bias(na_rpb, kh):
    w = np.arange(GRID_W)
    col_start = np.clip(w - NA_KW // 2, 0, GRID_W - NA_KW)
    col_ok = (w[None, :] >= col_start[:, None]) & (w[None, :] < col_start[:, None] + NA_KW)
    coff = np.clip(w[None, :] - w[:, None], -(NA_KW - 1), NA_KW - 1) + (NA_KW - 1)
    onehot = jnp.asarray(coff[:, :, None] == np.arange(2 * NA_KW - 1), F32)
    toe = jnp.einsum('dhrc,qkc->dhrqk', na_rpb.astype(F32), onehot, precision=lax.Precision.HIGHEST)
    toe = jnp.where(col_ok, toe, NEG)
    per_id = [toe[:, :, NA_KH - 1 - i:NA_KH - 1 - i + kh] for i in range(kh)]
    bias = jnp.stack(per_id, axis=2).transpose(0, 1, 2, 4, 3, 5)
    return bias.reshape(na_rpb.shape[0], na_rpb.shape[1], kh, GRID_W, kh * GRID_W)


def kernel(x, c, ctx, c_ctx, w_ada, b_ada, w_in, hgrn_lb, hgrn_norm_w, diff_lam, diff_norm_w, na_rpb,
           mlstm_f_bias, mlstm_norm_w, w_out, w_up, conv_w, conv_b, w_down, final_norm_w):
    bsz, n_lat, d = x.shape
    n_ctx = ctx.shape[1]
    depth = w_in.shape[0]
    d_ff = w_down.shape[1]
    grid_rows = n_lat // GRID_W

    xs = jnp.concatenate([ctx, x], axis=1)
    n_mod = -(-(bsz + 1) // 8) * 8
    cc = jnp.zeros((n_mod, d), F32).at[:bsz].set(c).at[n_mod - 1].set(c_ctx)
    mods = _ada(cc, w_ada, b_ada).reshape(depth, n_mod, 6, d)

    w_in_b = w_in.astype(BF16)
    w_gate_b = jnp.pad(w_in[:, :, D_MAIN:], ((0, 0), (0, 0), (0, HEAD_W - N_GATES))).astype(BF16)
    w_out_b = w_out.astype(BF16).reshape(depth, 4, GROUP_W, d)
    w_up_b = w_up.astype(BF16)
    w_down_b = w_down.astype(BF16)
    conv_b3 = conv_b.reshape(depth, 1, 2 * d_ff)

    lb = jnp.cumsum(jax.nn.softmax(hgrn_lb.astype(F32), axis=0), axis=0)
    lb = lb - lb[:1]
    zero = jnp.zeros_like(lb[:, :1])
    gate_par = jnp.concatenate([jnp.log(lb[:, 0:1]), jnp.log1p(-lb[:, 0:1]), 1.0 - lb[:, 0:1],
                                jnp.log(lb[:, 1:2]), jnp.log1p(-lb[:, 1:2]), 1.0 - lb[:, 1:2],
                                zero, zero], axis=1)
    cos, sin_signed = _rope_tables(n_ctx, n_lat)
    lam = jnp.pad(diff_lam.astype(F32), ((0, 0), (0, 4), (0, HEAD_W - diff_lam.shape[2])))
    bias = _na_bias(na_rpb, min(NA_KH, grid_rows))
    f_bias = mlstm_f_bias.astype(F32)

    def nw(w):
        return w.astype(F32).reshape(depth, 1, GROUP_W)

    hgrn_nw, diff_nw, mlstm_nw = nw(hgrn_norm_w), nw(diff_norm_w), nw(mlstm_norm_w)

    for l in range(depth):
        p, gates = _inproj(xs, mods, w_in_b, w_gate_b, l, n_ctx)
        g4 = gates[:, :, :N_GATES].reshape(bsz, -1, 4, HEADS)
        grow = jnp.pad(g4.transpose(0, 3, 2, 1), ((0, 0), (0, 0), (0, 4), (0, 0)))
        ya = _hgrn(p, gate_par, hgrn_nw, l, n_ctx)
        yb = _diff(p, cos, sin_signed, lam, diff_nw, l, n_ctx, _lambda_init(l))
        yc = _na(p, bias, l, n_ctx)
        yd = _mlstm(p, grow, f_bias[l], mlstm_nw, l, n_ctx)
        xs = _outproj((ya, yb, yc, yd), w_out_b, xs, mods, l, n_ctx)
        xs = _ffn(xs, mods, w_up_b, conv_w, conv_b3, w_down_b, l, n_ctx)
    return _final_norm(xs, final_norm_w, n_ctx)
```

```python
import functools
import math

import numpy as np
import jax
import jax.numpy as jnp
from jax import lax
from jax.experimental import pallas as pl
from jax.experimental.pallas import tpu as pltpu

F32 = jnp.float32
BF16 = jnp.bfloat16

HEADS = 4
HEAD_W = 128
GROUP_W = HEADS * HEAD_W
GRID_W = 64
NA_KH = 8
NA_KW = 16
ROPE_THETA = 10000.0
EPS = 1e-6
CONV_W = 3
D_MAIN = 15 * GROUP_W
N_GATES = 4 * HEADS

VMEM_LIMIT_BYTES = 56 * 1024 * 1024
NEG = -1e30
LOG2E = 1.4426950408889634

HGRN_CHUNK = 64
HGRN_SUB = 16
HGRN_UNROLL = 4
MLSTM_CHUNK = 128
MLSTM_UNROLL = 3
NA_UNROLL = 16
ROW_TILE = 768
INPROJ_TILE = 1280
FF_TILE = 512
HALO = 16
FFN_ROW_TILE = 768
FFN_SUB = 256
MOD_SUB = 256


def _unroll(n, target):
    return max(u for u in range(1, target + 1) if n % u == 0)


def _cparams(*sem):
    return pltpu.CompilerParams(dimension_semantics=sem, vmem_limit_bytes=VMEM_LIMIT_BYTES)


def _dot(a, b):
    return jnp.dot(a, b, preferred_element_type=F32)


def _dot_nt(a, b):
    return lax.dot_general(a, b, (((1,), (1,)), ((), ())), preferred_element_type=F32)


def _dot_tn(a, b):
    return lax.dot_general(a, b, (((0,), (0,)), ((), ())), preferred_element_type=F32)


def _silu(t):
    return t * jax.nn.sigmoid(t)


def _log_sigmoid(z):
    return jnp.minimum(z, 0.0) - jnp.log1p(jnp.exp(-jnp.abs(z)))


def _rms(t):
    return t * lax.rsqrt(jnp.mean(t * t, axis=-1, keepdims=True) + EPS)


def _split3(t):
    hi = t.astype(BF16)
    r1 = t - hi.astype(F32)
    mid = r1.astype(BF16)
    lo = (r1 - mid.astype(F32)).astype(BF16)
    return hi, mid, lo


def _modulate(x, mx_ref, mc_ref, row0, k, n_ctx):
    rows = row0 + lax.broadcasted_iota(jnp.int32, (x.shape[0], 1), 0)
    is_ctx = rows < n_ctx
    shift = jnp.where(is_ctx, mc_ref[k:k + 1, :], mx_ref[k:k + 1, :])
    scale = jnp.where(is_ctx, mc_ref[k + 1:k + 2, :], mx_ref[k + 1:k + 2, :])
    return _rms(x) * (1.0 + scale) + shift


def _modulate_into(dst_ref, dst_off, x_ref, mx_ref, mc_ref, row0, k, n_ctx):
    n = x_ref.shape[0]
    sub = math.gcd(n, MOD_SUB)
    for r in range(0, n, sub):
        h = _modulate(x_ref[r:r + sub, :], mx_ref, mc_ref, row0 + r, k, n_ctx)
        dst_ref[dst_off + r:dst_off + r + sub, :] = h.astype(dst_ref.dtype)


def _ada_kernel(c_ref, w_ref, b_ref, o_ref):
    s = _silu(c_ref[...]).astype(BF16)
    o_ref[...] = _dot(s, w_ref[...].astype(BF16)) + b_ref[...]


def _ada(cc, w_ada, b_ada):
    depth, d, n = w_ada.shape
    tn = 1024
    return pl.pallas_call(
        _ada_kernel,
        grid=(depth, n // tn),
        in_specs=[
            pl.BlockSpec((8, d), lambda l, j: (0, 0)),
            pl.BlockSpec((None, d, tn), lambda l, j: (l, 0, j)),
            pl.BlockSpec((None, 1, tn), lambda l, j: (l, 0, j)),
        ],
        out_specs=pl.BlockSpec((None, 8, tn), lambda l, j: (l, 0, j)),
        out_shape=jax.ShapeDtypeStruct((depth, 8, n), F32),
        compiler_params=_cparams("arbitrary", "arbitrary"),
    )(cc, w_ada, b_ada.reshape(depth, 1, n))


def _inproj_kernel(x_ref, mx_ref, mc_ref, w_ref, wg_ref, p_ref, gate_ref, h_ref, *, n_ctx, tm):
    i = pl.program_id(1)
    j = pl.program_id(2)

    @pl.when(j == 0)
    def _():
        _modulate_into(h_ref, 0, x_ref, mx_ref, mc_ref, i * tm, 0, n_ctx)
        gate_ref[...] = _dot(h_ref[...], wg_ref[...])

    p_ref[...] = _dot(h_ref[...], w_ref[...])


def _inproj(xs, mods, w_in, w_gate, l, n_ctx):
    b, t, d = xs.shape
    tm, tn = ROW_TILE, INPROJ_TILE
    nb = mods.shape[1] - 1
    return pl.pallas_call(
        functools.partial(_inproj_kernel, n_ctx=n_ctx, tm=tm),
        grid=(b, t // tm, D_MAIN // tn),
        in_specs=[
            pl.BlockSpec((None, tm, d), lambda bi, i, j: (bi, i, 0)),
            pl.BlockSpec((None, None, 6, d), lambda bi, i, j: (l, bi, 0, 0)),
            pl.BlockSpec((None, None, 6, d), lambda bi, i, j: (l, nb, 0, 0)),
            pl.BlockSpec((d, tn), lambda bi, i, j: (0, j)),
            pl.BlockSpec((None, d, HEAD_W), lambda bi, i, j: (l, 0, 0)),
        ],
        out_specs=[
            pl.BlockSpec((None, tm, tn), lambda bi, i, j: (bi, i, j)),
            pl.BlockSpec((None, tm, HEAD_W), lambda bi, i, j: (bi, i, 0)),
        ],
        out_shape=[
            jax.ShapeDtypeStruct((b, t, D_MAIN), F32),
            jax.ShapeDtypeStruct((b, t, HEAD_W), F32),
        ],
        scratch_shapes=[pltpu.VMEM((tm, d), BF16)],
        compiler_params=_cparams("arbitrary", "arbitrary", "arbitrary"),
    )(xs, mods, mods, w_in, w_gate)


def _mixer_call(kernel_fn, grid, in_specs, out_spec, out_shape, scratch_shapes, operands,
                cast_src=None, cast_layer=0):
    sem = ("arbitrary",) * len(grid)
    if cast_src is None:
        y = pl.pallas_call(kernel_fn, grid=grid, in_specs=in_specs, out_specs=out_spec, out_shape=out_shape,
                           scratch_shapes=scratch_shapes, compiler_params=_cparams(*sem))(*operands)
        return y, None
    n_in = len(in_specs)
    nblk = grid[0] * grid[1]
    rows, cols = cast_src.shape[1:]
    assert rows % (nblk * 16) == 0, (rows, nblk)
    rb = rows // nblk
    src_spec = pl.BlockSpec((None, rb, cols), lambda bi, h, *r: (cast_layer, bi * grid[1] + h, 0))
    dst_spec = pl.BlockSpec((rb, cols), lambda bi, h, *r: (bi * grid[1] + h, 0))

    def kern(*refs):
        src_ref, y_ref, dst_ref = refs[n_in], refs[n_in + 1], refs[n_in + 2]

        def convert():
            dst_ref[...] = src_ref[...].astype(dst_ref.dtype)

        if len(grid) > 2:
            pl.when(pl.program_id(2) == 0)(convert)
        else:
            convert()
        kernel_fn(*refs[:n_in], y_ref, *refs[n_in + 3:])

    return pl.pallas_call(
        kern, grid=grid, in_specs=list(in_specs) + [src_spec], out_specs=[out_spec, dst_spec],
        out_shape=[out_shape, jax.ShapeDtypeStruct((rows, cols), BF16)],
        scratch_shapes=scratch_shapes, compiler_params=_cparams(*sem))(*operands, cast_src)


def _round_robin(gens):
    live = list(gens)
    while live:
        nxt = []
        for g in live:
            try:
                next(g)
                nxt.append(g)
            except StopIteration:
                pass
        live = nxt


def _hgrn_chunk(q_ref, v_ref, z_ref, gp_ref, state, o_ref, r0, rev):
    L, SC = HGRN_CHUNK, HGRN_SUB
    nsub = L // SC
    rows = pl.ds(r0, L)
    q = _silu(q_ref[rows, :]) * (HEAD_W ** -0.5)
    z = z_ref[rows, :]
    v = v_ref[rows, :].astype(BF16)
    e = jnp.exp(-jnp.abs(z))
    inv = 1.0 / (1.0 + e)
    a = gp_ref[1:2, :] + (jnp.minimum(z, 0.0) - jnp.log(1.0 + e))
    ll = gp_ref[0:1, :]
    lf = jnp.maximum(ll, a) + jnp.log(1.0 + jnp.exp(-jnp.abs(ll - a)))
    k = gp_ref[2:3, :] * (jnp.where(z >= 0.0, e, 1.0) * inv)

    t_i = lax.broadcasted_iota(jnp.int32, (L, L), 0)
    s_i = lax.broadcasted_iota(jnp.int32, (L, L), 1)
    t_blk = t_i & ~(SC - 1)
    if not rev:
        local = (s_i <= t_i) & (s_i >= t_blk)
        causal = s_i <= t_i
    else:
        local = (s_i >= t_i) & (s_i < t_blk + SC)
        causal = s_i >= t_i
    sel = jnp.where(local, 1.0, 0.0).astype(BF16)
    hi, mid, lo = _split3(lf)
    bl = _dot(sel, hi) + _dot(sel, mid) + _dot(sel, lo)
    yield

    def blk(x, i):
        return x[i * SC:(i + 1) * SC]

    order = list(range(nsub)) if not rev else list(range(nsub - 1, -1, -1))
    r = [None] * nsub
    acc = jnp.zeros((1, HEAD_W), F32)
    for i in order:
        r[i] = acc
        acc = acc + (bl[(i + 1) * SC - 1:(i + 1) * SC] if not rev else bl[i * SC:i * SC + 1])
    b_end = acc

    qt = q * jnp.exp(bl)
    kinv = k * jnp.exp(-bl)
    kd = jnp.concatenate([blk(kinv, j) * jnp.exp(b_end - r[j]) for j in range(nsub)], axis=0).astype(BF16)
    upd = _dot_tn(v, kd)
    zero = jnp.zeros((SC, HEAD_W), F32)
    parts = []
    for i in range(nsub):
        seen = [j for j in range(nsub) if (j <= i if not rev else j >= i)]
        kk = jnp.concatenate(
            [(blk(kinv, j) if j == i else blk(kinv, j) * jnp.exp(r[i] - r[j])) if j in seen else zero
             for j in range(nsub)], axis=0).astype(BF16)
        parts.append(_dot_nt(blk(qt, i).astype(BF16), kk))
    qb = jnp.concatenate([blk(qt, i) * jnp.exp(r[i]) for i in range(nsub)], axis=0).astype(BF16)
    yield

    st = state[0]
    amat = jnp.where(causal, jnp.concatenate(parts, axis=0), 0.0)
    o_ref[rows, :] = _dot_nt(qb, st.astype(BF16)) + _dot(amat.astype(BF16), v)
    state[0] = st * jnp.exp(b_end) + upd


def _hgrn_kernel(q_ref, v_ref, zf_ref, zb_ref, g_ref, gp_ref, nw_ref, y_ref,
                 of_ref, ob_ref, sf_ref, sb_ref, *, n_ctx, n_tok):
    L = HGRN_CHUNK
    nc, ncc = n_tok // L, n_ctx // L
    U = _unroll(nc, HGRN_UNROLL)
    sf_ref[...] = jnp.zeros_like(sf_ref)
    sb_ref[...] = jnp.zeros_like(sb_ref)

    def body(it, carry):
        sf = [sf_ref[...]]
        sb = [sb_ref[...]]
        gens = []
        for u in range(U):
            step = it * U + u
            gens.append(_hgrn_chunk(q_ref, v_ref, zf_ref, gp_ref.at[0:3], sf, of_ref,
                                    pl.multiple_of(step * L, L), False))
            cb = jnp.where(step < ncc, ncc - 1 - step, nc - 1 - (step - ncc))
            gens.append(_hgrn_chunk(q_ref, v_ref, zb_ref, gp_ref.at[3:6], sb, ob_ref,
                                    pl.multiple_of(cb * L, L), True))
        _round_robin(gens)
        sf_ref[...] = sf[0]
        sb_ref[...] = sb[0]
        return carry

    lax.fori_loop(0, nc // U, body, 0)

    fr = 256
    def fin(i, carry):
        rows = pl.ds(pl.multiple_of(i * fr, fr), fr)
        o = of_ref[rows, :] + ob_ref[rows, :]
        y_ref[rows, :] = (_rms(o) * nw_ref[...] * _silu(g_ref[rows, :])).astype(y_ref.dtype)
        return carry

    lax.fori_loop(0, n_tok // fr, fin, 0)


def _hgrn(p, gate_par, norm_w, l, n_ctx, cast_src=None):
    b, t, _ = p.shape
    nh = GROUP_W // HEAD_W

    def col(seg):
        return pl.BlockSpec((None, t, HEAD_W), lambda bi, h: (bi, 0, seg * nh + h))

    return _mixer_call(
        functools.partial(_hgrn_kernel, n_ctx=n_ctx, n_tok=t),
        (b, HEADS),
        [col(0), col(1), col(2), col(3), col(4),
         pl.BlockSpec((None, 8, HEAD_W), lambda bi, h: (l, 0, h)),
         pl.BlockSpec((None, 1, HEAD_W), lambda bi, h: (l, 0, h))],
        pl.BlockSpec((None, t, HEAD_W), lambda bi, h: (bi, 0, h)),
        jax.ShapeDtypeStruct((b, t, GROUP_W), BF16),
        [pltpu.VMEM((t, HEAD_W), F32), pltpu.VMEM((t, HEAD_W), F32),
         pltpu.VMEM((HEAD_W, HEAD_W), F32), pltpu.VMEM((HEAD_W, HEAD_W), F32)],
        (p, p, p, p, p, gate_par, norm_w), cast_src, l + 1)


def _rope(t, cos, sin_signed):
    lane = lax.broadcasted_iota(jnp.int32, t.shape, 1)
    first = (lane & 31) < 16
    rot = jnp.where(first, pltpu.roll(t, HEAD_W - 16, 1), pltpu.roll(t, 16, 1))
    return t * cos + rot * sin_signed


def _softmax_rows(s):
    e = jnp.exp(s - jnp.max(s, axis=-1, keepdims=True))
    return e * (1.0 / jnp.sum(e, axis=-1, keepdims=True))


def _diff_kernel(q_ref, k_ref, v_ref, cos_ref, sin_ref, lam_ref, nw_ref, y_ref, kb_ref, vb_ref,
                 *, n_ctx, tq, lam_init):
    qi = pl.program_id(2)

    @pl.when(qi == 0)
    def _():
        kb_ref[...] = _rope(k_ref[...], cos_ref[...], sin_ref[...]).astype(BF16)
        v = v_ref[...]
        vb_ref[...] = jnp.concatenate([v, jnp.ones_like(v)], axis=1).astype(BF16)

    lam = lam_ref[...]
    lam_full = (jnp.exp(jnp.sum(lam[0:1] * lam[1:2], axis=-1, keepdims=True))
                - jnp.exp(jnp.sum(lam[2:3] * lam[3:4], axis=-1, keepdims=True)) + lam_init)
    rows = pl.ds(pl.multiple_of(qi * tq, tq), tq)
    q = _rope(q_ref[...], cos_ref[rows, :], sin_ref[rows, :]) * ((HEAD_W // 2) ** -0.5 * LOG2E)
    lane = lax.broadcasted_iota(jnp.int32, q.shape, 1)
    q0 = jnp.where(lane < HEAD_W // 2, q, 0.0).astype(BF16)
    q1 = jnp.where(lane >= HEAD_W // 2, q, 0.0).astype(BF16)

    def attend(kb, vb):
        s0 = _dot_nt(q0, kb)
        s1 = _dot_nt(q1, kb)
        e0 = jnp.exp2(s0 - jnp.max(s0, axis=-1, keepdims=True)).astype(BF16)
        e1 = jnp.exp2(s1 - jnp.max(s1, axis=-1, keepdims=True)).astype(BF16)
        t0 = _dot(e0, vb)
        t1 = _dot(e1, vb)
        o = t0[:, :HEAD_W] / t0[:, HEAD_W:] - lam_full * (t1[:, :HEAD_W] / t1[:, HEAD_W:])
        y_ref[...] = (_rms(o) * nw_ref[...] * (1.0 - lam_init)).astype(y_ref.dtype)

    @pl.when(qi == 0)
    def _():
        attend(kb_ref[0:n_ctx, :], vb_ref[0:n_ctx, :])

    @pl.when(qi > 0)
    def _():
        attend(kb_ref[...], vb_ref[...])


def _diff(p, cos, sin_signed, lam, norm_w, l, n_ctx, lam_init, cast_src=None):
    b, t, _ = p.shape
    tq = n_ctx
    base = 5 * HEADS

    def col(seg, rows, rmap):
        return pl.BlockSpec((None, rows, HEAD_W), lambda bi, h, qi: (bi, rmap(qi), base + seg * HEADS + h))

    return _mixer_call(
        functools.partial(_diff_kernel, n_ctx=n_ctx, tq=tq, lam_init=lam_init),
        (b, HEADS, t // tq),
        [col(0, tq, lambda qi: qi), col(1, t, lambda qi: 0), col(2, t, lambda qi: 0),
         pl.BlockSpec((t, HEAD_W), lambda bi, h, qi: (0, 0)),
         pl.BlockSpec((t, HEAD_W), lambda bi, h, qi: (0, 0)),
         pl.BlockSpec((None, 8, HEAD_W), lambda bi, h, qi: (l, 0, 0)),
         pl.BlockSpec((None, 1, HEAD_W), lambda bi, h, qi: (l, 0, h))],
        pl.BlockSpec((None, tq, HEAD_W), lambda bi, h, qi: (bi, qi, h)),
        jax.ShapeDtypeStruct((b, t, GROUP_W), BF16),
        [pltpu.VMEM((t, HEAD_W), BF16), pltpu.VMEM((t, 2 * HEAD_W), BF16)],
        (p, p, p, cos, sin_signed, lam, norm_w), cast_src, l + 1)


def _na_kernel(q_ref, k_ref, v_ref, bias_ref, y_ref, kb_ref, vb_ref, *, n_ctx, grid_rows, kh):
    scale = HEAD_W ** -0.5
    kb_ref[...] = k_ref[...].astype(BF16)
    vb_ref[...] = v_ref[...].astype(BF16)
    kc = kb_ref[0:n_ctx, :]
    vc = vb_ref[0:n_ctx, :]

    qc = (q_ref[0:n_ctx, :] * scale).astype(BF16)
    pc = _softmax_rows(_dot_nt(qc, kc))
    y_ref[0:n_ctx, :] = _dot(pc.astype(BF16), vc).astype(y_ref.dtype)

    win = kh * GRID_W

    def one_row(r, out):
        rows = pl.ds(pl.multiple_of(n_ctx + r * GRID_W, GRID_W), GRID_W)
        q = (q_ref[rows, :] * scale).astype(BF16)
        start = jnp.clip(r - kh // 2, 0, grid_rows - kh)
        wrows = pl.ds(pl.multiple_of(n_ctx + start * GRID_W, GRID_W), win)
        sb = _dot_nt(q, kb_ref[wrows, :]) + bias_ref[r - start]
        sc = _dot_nt(q, kc)
        yield
        m = jnp.maximum(jnp.max(sb, axis=-1, keepdims=True), jnp.max(sc, axis=-1, keepdims=True))
        eb = jnp.exp(sb - m)
        ec = jnp.exp(sc - m)
        inv = 1.0 / (jnp.sum(eb, axis=-1, keepdims=True) + jnp.sum(ec, axis=-1, keepdims=True))
        pb = (eb * inv).astype(BF16)
        pc_ = (ec * inv).astype(BF16)
        yield
        out.append(_dot(pb, vb_ref[wrows, :]) + _dot(pc_, vc))

    nu = _unroll(grid_rows, NA_UNROLL)

    def body(it, carry):
        outs = [[] for _ in range(nu)]
        _round_robin([one_row(it * nu + u, outs[u]) for u in range(nu)])
        o = jnp.concatenate([x[0] for x in outs], axis=0)
        rows = pl.ds(pl.multiple_of(n_ctx + it * (nu * GRID_W), GRID_W), nu * GRID_W)
        y_ref[rows, :] = o.astype(y_ref.dtype)
        return carry

    lax.fori_loop(0, grid_rows // nu, body, 0)


def _na(p, bias, l, n_ctx, cast_src=None):
    b, t, _ = p.shape
    grid_rows = (t - n_ctx) // GRID_W
    kh = min(NA_KH, grid_rows)
    base = 8 * HEADS

    def col(seg):
        return pl.BlockSpec((None, t, HEAD_W), lambda bi, h: (bi, 0, base + seg * HEADS + h))

    return _mixer_call(
        functools.partial(_na_kernel, n_ctx=n_ctx, grid_rows=grid_rows, kh=kh),
        (b, HEADS),
        [col(0), col(1), col(2),
         pl.BlockSpec((None, None, kh, GRID_W, kh * GRID_W), lambda bi, h: (l, h, 0, 0, 0))],
        pl.BlockSpec((None, t, HEAD_W), lambda bi, h: (bi, 0, h)),
        jax.ShapeDtypeStruct((b, t, GROUP_W), BF16),
        [pltpu.VMEM((t, HEAD_W), BF16), pltpu.VMEM((t, HEAD_W), BF16)],
        (p, p, p, bias), cast_src, l + 1)


def _mlstm_chunk(q_ref, k_ref, v_ref, gr_ref, fb, state, h_ref, r0, rev):
    L = MLSTM_CHUNK
    rows = pl.ds(r0, L)
    gi, gf = (2, 3) if rev else (0, 1)
    q = (q_ref[rows, :] * (HEAD_W ** -0.5)).astype(BF16)
    k = k_ref[rows, :]
    v = v_ref[rows, :]
    vaug = jnp.concatenate([v, jnp.ones_like(v)], axis=1).astype(BF16)
    grow = gr_ref[:, rows]
    i_row = grow[gi:gi + 1, :]
    f_rows = _log_sigmoid(grow + fb)

    t_i = lax.broadcasted_iota(jnp.int32, (L, L), 0)
    s_i = lax.broadcasted_iota(jnp.int32, (L, L), 1)
    causal = (s_i <= t_i) if not rev else (s_i >= t_i)
    b_col = jnp.sum(jnp.where(causal, f_rows[gf:gf + 1, :], 0.0), axis=1, keepdims=True)
    tri_t = jnp.where((t_i <= s_i) if not rev else (t_i >= s_i), 1.0, 0.0).astype(BF16)
    fr3 = _split3(f_rows)
    b_row = (_dot(fr3[0], tri_t) + _dot(fr3[1], tri_t) + _dot(fr3[2], tri_t))[gf:gf + 1, :]
    b_tot = b_row[:, L - 1:L] if not rev else b_row[:, 0:1]
    log_w = jnp.where(causal, b_col - b_row + i_row, NEG)
    row_max = jnp.max(log_w, axis=1, keepdims=True)
    upd_row = b_tot - b_row + i_row
    upd_max = jnp.max(upd_row, axis=1, keepdims=True)
    s_qk = _dot_nt(q, k.astype(BF16))
    k_t = k.T
    yield

    m = state[1]
    inter = b_col + m
    m_t = jnp.maximum(inter, row_max)
    w_inter = jnp.exp(inter - m_t)
    w_intra = (jnp.exp(log_w - m_t) * s_qk).astype(BF16)
    m_new = jnp.maximum(b_tot + m, upd_max)
    w_s = jnp.exp(b_tot + m - m_new)
    state[1] = m_new
    upd = _dot((k_t * jnp.exp(upd_row - m_new)).astype(BF16), vaug)
    intra = _dot(w_intra, vaug)
    yield

    c_aug = state[0]
    tot = w_inter * _dot(q, c_aug.astype(BF16)) + intra
    state[0] = w_s * c_aug + upd
    h_ref[rows, :] = tot[:, :HEAD_W] / jnp.maximum(jnp.abs(tot[:, HEAD_W:]), jnp.exp(-m_t))


def _mlstm_kernel(fb_ref, q_ref, k_ref, v_ref, og_ref, gr_ref, nw_ref, y_ref,
                  hf_ref, hb_ref, cf_ref, cb_ref, mf_ref, mb_ref, *, n_ctx, n_tok):
    L = MLSTM_CHUNK
    nc, ncc = n_tok // L, n_ctx // L
    U = _unroll(nc, MLSTM_UNROLL)
    h = pl.program_id(1)
    fb_f = fb_ref[0, h]
    fb_b = fb_ref[1, h]
    for ref in (cf_ref, cb_ref, mf_ref, mb_ref):
        ref[...] = jnp.zeros_like(ref)

    def body(it, carry):
        sf = [cf_ref[...], mf_ref[...]]
        sb = [cb_ref[...], mb_ref[...]]
        gens = []
        for u in range(U):
            step = it * U + u
            gens.append(_mlstm_chunk(q_ref, k_ref, v_ref, gr_ref, fb_f, sf, hf_ref,
                                     pl.multiple_of(step * L, L), False))
            ck = jnp.where(step < ncc, ncc - 1 - step, nc - 1 - (step - ncc))
            gens.append(_mlstm_chunk(q_ref, k_ref, v_ref, gr_ref, fb_b, sb, hb_ref,
                                     pl.multiple_of(ck * L, L), True))
        _round_robin(gens)
        cf_ref[...], mf_ref[...] = sf
        cb_ref[...], mb_ref[...] = sb
        return carry

    lax.fori_loop(0, nc // U, body, 0)

    fr = 256
    def fin(i, carry):
        rows = pl.ds(pl.multiple_of(i * fr, fr), fr)
        o = jax.nn.sigmoid(og_ref[rows, :]) * (hf_ref[rows, :] + hb_ref[rows, :])
        y_ref[rows, :] = (_rms(o) * nw_ref[...]).astype(y_ref.dtype)
        return carry

    lax.fori_loop(0, n_tok // fr, fin, 0)


def _mlstm(p, grow, f_bias, norm_w, l, n_ctx, cast_src=None):
    b, t, _ = p.shape
    base = 11 * HEADS

    def col(seg):
        return pl.BlockSpec((None, t, HEAD_W), lambda bi, h: (bi, 0, base + seg * HEADS + h))

    return _mixer_call(
        functools.partial(_mlstm_kernel, n_ctx=n_ctx, n_tok=t),
        (b, HEADS),
        [pl.BlockSpec(memory_space=pltpu.SMEM),
         col(0), col(1), col(2), col(3),
         pl.BlockSpec((None, None, 8, t), lambda bi, h: (bi, h, 0, 0)),
         pl.BlockSpec((None, 1, HEAD_W), lambda bi, h: (l, 0, h))],
        pl.BlockSpec((None, t, HEAD_W), lambda bi, h: (bi, 0, h)),
        jax.ShapeDtypeStruct((b, t, GROUP_W), BF16),
        [pltpu.VMEM((t, HEAD_W), F32), pltpu.VMEM((t, HEAD_W), F32),
         pltpu.VMEM((HEAD_W, 2 * HEAD_W), F32), pltpu.VMEM((HEAD_W, 2 * HEAD_W), F32),
         pltpu.VMEM((1, 1), F32), pltpu.VMEM((1, 1), F32)],
        (f_bias, p, p, p, p, grow, norm_w), cast_src, l + 1)


def _outproj_kernel(ya_ref, yb_ref, yc_ref, yd_ref, w_ref, x_ref, mx_ref, mc_ref, o_ref, *, n_ctx, tm):
    i = pl.program_id(1)
    acc = _dot(ya_ref[...], w_ref[0])
    acc += _dot(yb_ref[...], w_ref[1])
    acc += _dot(yc_ref[...], w_ref[2])
    acc += _dot(yd_ref[...], w_ref[3])
    rows = i * tm + lax.broadcasted_iota(jnp.int32, (tm, 1), 0)
    gate = jnp.where(rows < n_ctx, mc_ref[2:3, :], mx_ref[2:3, :])
    o_ref[...] = x_ref[...] + gate * acc


def _outproj(ys, w_out, xs, mods, l, n_ctx):
    b, t, d = xs.shape
    tm, tn = ROW_TILE, 1024
    nb = mods.shape[1] - 1
    yspec = pl.BlockSpec((None, tm, GROUP_W), lambda bi, i, j: (bi, i, 0))
    return pl.pallas_call(
        functools.partial(_outproj_kernel, n_ctx=n_ctx, tm=tm),
        grid=(b, t // tm, d // tn),
        in_specs=[yspec, yspec, yspec, yspec,
                  pl.BlockSpec((4, GROUP_W, tn), lambda bi, i, j: (0, 0, j)),
                  pl.BlockSpec((None, tm, tn), lambda bi, i, j: (bi, i, j)),
                  pl.BlockSpec((None, None, 6, tn), lambda bi, i, j: (l, bi, 0, j)),
                  pl.BlockSpec((None, None, 6, tn), lambda bi, i, j: (l, nb, 0, j))],
        out_specs=pl.BlockSpec((None, tm, tn), lambda bi, i, j: (bi, i, j)),
        out_shape=jax.ShapeDtypeStruct((b, t, d), F32),
        compiler_params=_cparams("arbitrary", "arbitrary", "arbitrary"),
    )(*ys, w_out, xs, mods, mods)


def _ffn_kernel(xm_ref, xp_ref, xn_ref, mx_ref, mc_ref, wa_ref, wg_ref, cwa_ref, cwg_ref,
                cba_ref, cbg_ref, wd_ref, o_ref, hs_ref, ua_ref, ug_ref, *, n_ctx, n_tok, tm):
    i = pl.program_id(1)
    f = pl.program_id(2)
    nf = pl.num_programs(2) - 1
    row0 = i * tm
    cur = f % 2
    prev = 1 - cur

    def up_project():
        ua_ref[cur] = _dot(hs_ref[...], wa_ref[...])
        ug_ref[cur] = _dot(hs_ref[...], wg_ref[...])

    def finish():
        ua = ua_ref.at[prev]
        ug = ug_ref.at[prev]
        for r in range(0, tm, FFN_SUB):
            rows = row0 + r + lax.broadcasted_iota(jnp.int32, (FFN_SUB, 1), 0)
            has_prev = (rows != 0) & (rows != n_ctx)
            has_next = (rows != n_ctx - 1) & (rows != n_tok - 1)

            def conv(u_ref, cw_ref, cb_ref):
                lo = HALO + r
                before = jnp.where(has_prev, u_ref[lo - 1:lo - 1 + FFN_SUB, :], 0.0)
                after = jnp.where(has_next, u_ref[lo + 1:lo + 1 + FFN_SUB, :], 0.0)
                return (before * cw_ref[0:1, :] + u_ref[lo:lo + FFN_SUB, :] * cw_ref[1:2, :]
                        + after * cw_ref[2:3, :] + cb_ref[...])

            act = (_silu(conv(ua, cwa_ref, cba_ref)) * conv(ug, cwg_ref, cbg_ref)).astype(BF16)
            o_ref[r:r + FFN_SUB, :] += _dot(act, wd_ref[...])

    @pl.when(f == 0)
    def _():
        hs_ref[0:HALO, :] = _modulate(xp_ref[...], mx_ref, mc_ref, row0 - HALO, 3, n_ctx).astype(BF16)
        _modulate_into(hs_ref, HALO, xm_ref, mx_ref, mc_ref, row0, 3, n_ctx)
        hs_ref[HALO + tm:, :] = _modulate(xn_ref[...], mx_ref, mc_ref, row0 + tm, 3, n_ctx).astype(BF16)
        o_ref[...] = jnp.zeros_like(o_ref)
        up_project()

    @pl.when((f > 0) & (f < nf))
    def _():
        up_project()
        finish()

    @pl.when(f == nf)
    def _():
        finish()
        rows = row0 + lax.broadcasted_iota(jnp.int32, (tm, 1), 0)
        gate = jnp.where(rows < n_ctx, mc_ref[5:6, :], mx_ref[5:6, :])
        o_ref[...] = xm_ref[...] + gate * o_ref[...]


def _ffn(xs, mods, w_up, conv_w, conv_b, w_down, l, n_ctx):
    b, t, d = xs.shape
    tm, tf = FFN_ROW_TILE, FF_TILE
    d_ff = w_down.shape[0]
    nf = d_ff // tf
    nb = mods.shape[1] - 1
    hb = tm // HALO
    last = t // HALO - 1

    def up(f):
        return jnp.minimum(f, nf - 1)

    def fin(f):
        return jnp.maximum(f - 1, 0)

    return pl.pallas_call(
        functools.partial(_ffn_kernel, n_ctx=n_ctx, n_tok=t, tm=tm),
        grid=(b, t // tm, nf + 1),
        in_specs=[
            pl.BlockSpec((None, tm, d), lambda bi, i, f: (bi, i, 0)),
            pl.BlockSpec((None, HALO, d), lambda bi, i, f: (bi, jnp.maximum(i * hb - 1, 0), 0)),
            pl.BlockSpec((None, HALO, d), lambda bi, i, f: (bi, jnp.minimum((i + 1) * hb, last), 0)),
            pl.BlockSpec((None, None, 6, d), lambda bi, i, f: (l, bi, 0, 0)),
            pl.BlockSpec((None, None, 6, d), lambda bi, i, f: (l, nb, 0, 0)),
            pl.BlockSpec((d, tf), lambda bi, i, f: (0, up(f))),
            pl.BlockSpec((d, tf), lambda bi, i, f: (0, nf + up(f))),
            pl.BlockSpec((None, CONV_W, tf), lambda bi, i, f: (l, 0, fin(f))),
            pl.BlockSpec((None, CONV_W, tf), lambda bi, i, f: (l, 0, nf + fin(f))),
            pl.BlockSpec((None, 1, tf), lambda bi, i, f: (l, 0, fin(f))),
            pl.BlockSpec((None, 1, tf), lambda bi, i, f: (l, 0, nf + fin(f))),
            pl.BlockSpec((tf, d), lambda bi, i, f: (fin(f), 0)),
        ],
        out_specs=pl.BlockSpec((None, tm, d), lambda bi, i, f: (bi, i, 0)),
        out_shape=jax.ShapeDtypeStruct((b, t, d), F32),
        scratch_shapes=[pltpu.VMEM((tm + 2 * HALO, d), BF16),
                        pltpu.VMEM((2, tm + 2 * HALO, tf), F32),
                        pltpu.VMEM((2, tm + 2 * HALO, tf), F32)],
        compiler_params=_cparams("arbitrary", "arbitrary", "arbitrary"),
    )(xs, xs, xs, mods, mods, w_up, w_up, conv_w, conv_w, conv_b, conv_b, w_down)


def _final_kernel(x_ref, w_ref, o_ref):
    o_ref[...] = _rms(x_ref[...]) * w_ref[...]


def _final_norm(xs, w, n_ctx):
    b, t, d = xs.shape
    tr = n_ctx
    off = n_ctx // tr
    return pl.pallas_call(
        _final_kernel,
        grid=(b, (t - n_ctx) // tr),
        in_specs=[pl.BlockSpec((None, tr, d), lambda bi, i: (bi, i + off, 0)),
                  pl.BlockSpec((1, d), lambda bi, i: (0, 0))],
        out_specs=pl.BlockSpec((None, tr, d), lambda bi, i: (bi, i, 0)),
        out_shape=jax.ShapeDtypeStruct((b, t - n_ctx, d), F32),
        compiler_params=_cparams("arbitrary", "arbitrary"),
    )(xs, w.reshape(1, d))


def _lambda_init(layer_idx):
    return 0.8 - 0.6 * math.exp(-0.3 * layer_idx)


def _rope_tables(n_ctx, n_lat):
    dh = HEAD_W // 2
    t = np.arange(n_lat)
    row = (t // GRID_W).astype(np.float32)
    colp = (t % GRID_W).astype(np.float32)
    half = dh // 2
    inv = (ROPE_THETA ** (-np.arange(0, half, 2, dtype=np.float32) / half)).astype(np.float32)
    ar = row[:, None] * inv
    ac = colp[:, None] * inv
    ang = jnp.asarray(np.concatenate([ar, ar, ac, ac], axis=-1))
    cos, sin = jnp.cos(ang), jnp.sin(ang)
    sign = np.where((np.arange(dh) % 32) < 16, -1.0, 1.0).astype(np.float32)
    cos = jnp.concatenate([jnp.ones((n_ctx, dh), F32), cos], axis=0)
    sin = jnp.concatenate([jnp.zeros((n_ctx, dh), F32), sin * sign], axis=0)
    return jnp.tile(cos, (1, 2)), jnp.tile(sin, (1, 2))


def _na_bias(na_rpb, kh):
    w = np.arange(GRID_W)
    col_start = np.clip(w - NA_KW // 2, 0, GRID_W - NA_KW)
    col_ok = (w[None, :] >= col_start[:, None]) & (w[None, :] < col_start[:, None] + NA_KW)
    coff = np.clip(w[None, :] - w[:, None], -(NA_KW - 1), NA_KW - 1) + (NA_KW - 1)
    onehot = jnp.asarray(coff[:, :, None] == np.arange(2 * NA_KW - 1), F32)
    toe = jnp.einsum('dhrc,qkc->dhrqk', na_rpb.astype(F32), onehot, precision=lax.Precision.HIGHEST)
    toe = jnp.where(col_ok, toe, NEG)
    per_id = [toe[:, :, NA_KH - 1 - i:NA_KH - 1 - i + kh] for i in range(kh)]
    bias = jnp.stack(per_id, axis=2).transpose(0, 1, 2, 4, 3, 5)
    return bias.reshape(na_rpb.shape[0], na_rpb.shape[1], kh, GRID_W, kh * GRID_W)


def kernel(x, c, ctx, c_ctx, w_ada, b_ada, w_in, hgrn_lb, hgrn_norm_w, diff_lam, diff_norm_w, na_rpb,
           mlstm_f_bias, mlstm_norm_w, w_out, w_up, conv_w, conv_b, w_down, final_norm_w):
    bsz, n_lat, d = x.shape
    n_ctx = ctx.shape[1]
    depth = w_in.shape[0]
    d_ff = w_down.shape[1]
    grid_rows = n_lat // GRID_W

    xs = jnp.concatenate([ctx, x], axis=1)
    n_mod = -(-(bsz + 1) // 8) * 8
    cc = jnp.zeros((n_mod, d), F32).at[:bsz].set(c).at[n_mod - 1].set(c_ctx)
    mods = _ada(cc, w_ada, b_ada).reshape(depth, n_mod, 6, d)

    w_in_b, w_out_b, w_up_b, w_down_b = (w[0].astype(BF16) for w in (w_in, w_out, w_up, w_down))
    w_gate_b = jnp.pad(w_in[:, :, D_MAIN:], ((0, 0), (0, 0), (0, HEAD_W - N_GATES))).astype(BF16)
    conv_b3 = conv_b.reshape(depth, 1, 2 * d_ff)

    lb = jnp.cumsum(jax.nn.softmax(hgrn_lb.astype(F32), axis=0), axis=0)
    lb = lb - lb[:1]
    zero = jnp.zeros_like(lb[:, :1])
    gate_par = jnp.concatenate([jnp.log(lb[:, 0:1]), jnp.log1p(-lb[:, 0:1]), 1.0 - lb[:, 0:1],
                                jnp.log(lb[:, 1:2]), jnp.log1p(-lb[:, 1:2]), 1.0 - lb[:, 1:2],
                                zero, zero], axis=1)
    cos, sin_signed = _rope_tables(n_ctx, n_lat)
    lam = jnp.pad(diff_lam.astype(F32), ((0, 0), (0, 4), (0, HEAD_W - diff_lam.shape[2])))
    bias = _na_bias(na_rpb, min(NA_KH, grid_rows))
    f_bias = mlstm_f_bias.astype(F32)

    def nw(w):
        return w.astype(F32).reshape(depth, 1, GROUP_W)

    hgrn_nw, diff_nw, mlstm_nw = nw(hgrn_norm_w), nw(diff_norm_w), nw(mlstm_norm_w)

    for l in range(depth):
        p, gates = _inproj(xs, mods, w_in_b, w_gate_b, l, n_ctx)
        g4 = gates[:, :, :N_GATES].reshape(bsz, -1, 4, HEADS)
        grow = jnp.pad(g4.transpose(0, 3, 2, 1), ((0, 0), (0, 0), (0, 4), (0, 0)))
        nxt = l + 1 < depth
        ya, w_up_n = _hgrn(p, gate_par, hgrn_nw, l, n_ctx, w_up if nxt else None)
        yb, w_in_n = _diff(p, cos, sin_signed, lam, diff_nw, l, n_ctx, _lambda_init(l), w_in if nxt else None)
        yc, w_down_n = _na(p, bias, l, n_ctx, w_down if nxt else None)
        yd, w_out_n = _mlstm(p, grow, f_bias[l], mlstm_nw, l, n_ctx, w_out if nxt else None)
        xs = _outproj((ya, yb, yc, yd), w_out_b.reshape(4, GROUP_W, d), xs, mods, l, n_ctx)
        xs = _ffn(xs, mods, w_up_b, conv_w, conv_b3, w_down_b, l, n_ctx)
        w_in_b, w_out_b, w_up_b, w_down_b = w_in_n, w_out_n, w_up_n, w_down_n
    return _final_norm(xs, final_norm_w, n_ctx)
```

```python
import functools
import math

import numpy as np
import jax
import jax.numpy as jnp
from jax import lax
from jax.experimental import pallas as pl
from jax.experimental.pallas import tpu as pltpu

F32 = jnp.float32
BF16 = jnp.bfloat16

HEADS = 4
HEAD_W = 128
GROUP_W = HEADS * HEAD_W
GRID_W = 64
NA_KH = 8
NA_KW = 16
ROPE_THETA = 10000.0
EPS = 1e-6
CONV_W = 3
D_MAIN = 15 * GROUP_W
N_GATES = 4 * HEADS

VMEM_LIMIT_BYTES = 56 * 1024 * 1024
NEG = -1e30
LOG2E = 1.4426950408889634

HGRN_CHUNK = 64
HGRN_SUB = 16
HGRN_UNROLL = 4
MLSTM_CHUNK = 128
MLSTM_UNROLL = 3
NA_UNROLL = 16
ROW_TILE = 768
INPROJ_TILE = 1280
FF_TILE = 512
HALO = 16
FFN_ROW_TILE = 768
FFN_SUB = 256
MOD_SUB = 256


def _unroll(n, target):
    return max(u for u in range(1, target + 1) if n % u == 0)


def _cparams(*sem):
    return pltpu.CompilerParams(dimension_semantics=sem, vmem_limit_bytes=VMEM_LIMIT_BYTES)


def _dot(a, b):
    return jnp.dot(a, b, preferred_element_type=F32)


def _dot_nt(a, b):
    return lax.dot_general(a, b, (((1,), (1,)), ((), ())), preferred_element_type=F32)


def _dot_tn(a, b):
    return lax.dot_general(a, b, (((0,), (0,)), ((), ())), preferred_element_type=F32)


def _silu(t):
    return t * jax.nn.sigmoid(t)


def _log_sigmoid(z):
    return jnp.minimum(z, 0.0) - jnp.log1p(jnp.exp(-jnp.abs(z)))


def _rms(t):
    return t * lax.rsqrt(jnp.mean(t * t, axis=-1, keepdims=True) + EPS)


def _split3(t):
    hi = t.astype(BF16)
    r1 = t - hi.astype(F32)
    mid = r1.astype(BF16)
    lo = (r1 - mid.astype(F32)).astype(BF16)
    return hi, mid, lo


def _modulate(x, mx_ref, mc_ref, row0, k, n_ctx):
    rows = row0 + lax.broadcasted_iota(jnp.int32, (x.shape[0], 1), 0)
    is_ctx = rows < n_ctx
    shift = jnp.where(is_ctx, mc_ref[k:k + 1, :], mx_ref[k:k + 1, :])
    scale = jnp.where(is_ctx, mc_ref[k + 1:k + 2, :], mx_ref[k + 1:k + 2, :])
    return _rms(x) * (1.0 + scale) + shift


def _modulate_into(dst_ref, dst_off, x_ref, mx_ref, mc_ref, row0, k, n_ctx):
    n = x_ref.shape[0]
    sub = math.gcd(n, MOD_SUB)
    for r in range(0, n, sub):
        h = _modulate(x_ref[r:r + sub, :], mx_ref, mc_ref, row0 + r, k, n_ctx)
        dst_ref[dst_off + r:dst_off + r + sub, :] = h.astype(dst_ref.dtype)


def _ada_kernel(c_ref, w_ref, b_ref, o_ref):
    s = _silu(c_ref[...]).astype(BF16)
    o_ref[...] = _dot(s, w_ref[...].astype(BF16)) + b_ref[...]


def _ada(cc, w_ada, b_ada):
    depth, d, n = w_ada.shape
    tn = 1024
    return pl.pallas_call(
        _ada_kernel,
        grid=(depth, n // tn),
        in_specs=[
            pl.BlockSpec((8, d), lambda l, j: (0, 0)),
            pl.BlockSpec((None, d, tn), lambda l, j: (l, 0, j)),
            pl.BlockSpec((None, 1, tn), lambda l, j: (l, 0, j)),
        ],
        out_specs=pl.BlockSpec((None, 8, tn), lambda l, j: (l, 0, j)),
        out_shape=jax.ShapeDtypeStruct((depth, 8, n), F32),
        compiler_params=_cparams("arbitrary", "arbitrary"),
    )(cc, w_ada, b_ada.reshape(depth, 1, n))


def _inproj_kernel(x_ref, mx_ref, mc_ref, w_ref, wg_ref, p_ref, gate_ref, h_ref, *, n_ctx, tm):
    i = pl.program_id(1)
    j = pl.program_id(2)

    @pl.when(j == 0)
    def _():
        _modulate_into(h_ref, 0, x_ref, mx_ref, mc_ref, i * tm, 0, n_ctx)
        gate_ref[...] = _dot(h_ref[...], wg_ref[...])

    p_ref[...] = _dot(h_ref[...], w_ref[...])


def _inproj(xs, mods, w_in, w_gate, l, n_ctx):
    b, t, d = xs.shape
    tm, tn = ROW_TILE, INPROJ_TILE
    nb = mods.shape[1] - 1
    return pl.pallas_call(
        functools.partial(_inproj_kernel, n_ctx=n_ctx, tm=tm),
        grid=(b, t // tm, D_MAIN // tn),
        in_specs=[
            pl.BlockSpec((None, tm, d), lambda bi, i, j: (bi, i, 0)),
            pl.BlockSpec((None, None, 6, d), lambda bi, i, j: (l, bi, 0, 0)),
            pl.BlockSpec((None, None, 6, d), lambda bi, i, j: (l, nb, 0, 0)),
            pl.BlockSpec((None, d, tn), lambda bi, i, j: (l, 0, j)),
            pl.BlockSpec((None, d, HEAD_W), lambda bi, i, j: (l, 0, 0)),
        ],
        out_specs=[
            pl.BlockSpec((None, tm, tn), lambda bi, i, j: (bi, i, j)),
            pl.BlockSpec((None, tm, HEAD_W), lambda bi, i, j: (bi, i, 0)),
        ],
        out_shape=[
            jax.ShapeDtypeStruct((b, t, D_MAIN), F32),
            jax.ShapeDtypeStruct((b, t, HEAD_W), F32),
        ],
        scratch_shapes=[pltpu.VMEM((tm, d), BF16)],
        compiler_params=_cparams("arbitrary", "arbitrary", "arbitrary"),
    )(xs, mods, mods, w_in, w_gate)


def _mixer_call(kernel_fn, grid, in_specs, out_spec, out_shape, scratch_shapes, operands,
                cast_src=None, cast_layer=0):
    sem = ("arbitrary",) * len(grid)
    if cast_src is None:
        y = pl.pallas_call(kernel_fn, grid=grid, in_specs=in_specs, out_specs=out_spec, out_shape=out_shape,
                           scratch_shapes=scratch_shapes, compiler_params=_cparams(*sem))(*operands)
        return y, None
    n_in = len(in_specs)
    nblk = grid[0] * grid[1]
    rows, cols = cast_src.shape[1:]
    assert rows % (nblk * 16) == 0, (rows, nblk)
    rb = rows // nblk
    src_spec = pl.BlockSpec((None, rb, cols), lambda bi, h, *r: (cast_layer, bi * grid[1] + h, 0))
    dst_spec = pl.BlockSpec((rb, cols), lambda bi, h, *r: (bi * grid[1] + h, 0))

    def kern(*refs):
        src_ref, y_ref, dst_ref = refs[n_in], refs[n_in + 1], refs[n_in + 2]

        def convert():
            dst_ref[...] = src_ref[...].astype(dst_ref.dtype)

        if len(grid) > 2:
            pl.when(pl.program_id(2) == 0)(convert)
        else:
            convert()
        kernel_fn(*refs[:n_in], y_ref, *refs[n_in + 3:])

    return pl.pallas_call(
        kern, grid=grid, in_specs=list(in_specs) + [src_spec], out_specs=[out_spec, dst_spec],
        out_shape=[out_shape, jax.ShapeDtypeStruct((rows, cols), BF16)],
        scratch_shapes=scratch_shapes, compiler_params=_cparams(*sem))(*operands, cast_src)


def _round_robin(gens):
    live = list(gens)
    while live:
        nxt = []
        for g in live:
            try:
                next(g)
                nxt.append(g)
            except StopIteration:
                pass
        live = nxt


def _hgrn_chunk(q_ref, v_ref, z_ref, gp_ref, state, o_ref, r0, rev):
    L, SC = HGRN_CHUNK, HGRN_SUB
    nsub = L // SC
    rows = pl.ds(r0, L)
    q = _silu(q_ref[rows, :]) * (HEAD_W ** -0.5)
    z = z_ref[rows, :]
    v = v_ref[rows, :].astype(BF16)
    e = jnp.exp(-jnp.abs(z))
    inv = 1.0 / (1.0 + e)
    a = gp_ref[1:2, :] + (jnp.minimum(z, 0.0) - jnp.log(1.0 + e))
    ll = gp_ref[0:1, :]
    lf = jnp.maximum(ll, a) + jnp.log(1.0 + jnp.exp(-jnp.abs(ll - a)))
    k = gp_ref[2:3, :] * (jnp.where(z >= 0.0, e, 1.0) * inv)

    t_i = lax.broadcasted_iota(jnp.int32, (L, L), 0)
    s_i = lax.broadcasted_iota(jnp.int32, (L, L), 1)
    t_blk = t_i & ~(SC - 1)
    if not rev:
        local = (s_i <= t_i) & (s_i >= t_blk)
        causal = s_i <= t_i
    else:
        local = (s_i >= t_i) & (s_i < t_blk + SC)
        causal = s_i >= t_i
    sel = jnp.where(local, 1.0, 0.0).astype(BF16)
    hi, mid, lo = _split3(lf)
    bl = _dot(sel, hi) + _dot(sel, mid) + _dot(sel, lo)
    yield

    def blk(x, i):
        return x[i * SC:(i + 1) * SC]

    order = list(range(nsub)) if not rev else list(range(nsub - 1, -1, -1))
    r = [None] * nsub
    acc = jnp.zeros((1, HEAD_W), F32)
    for i in order:
        r[i] = acc
        acc = acc + (bl[(i + 1) * SC - 1:(i + 1) * SC] if not rev else bl[i * SC:i * SC + 1])
    b_end = acc

    qt = q * jnp.exp(bl)
    kinv = k * jnp.exp(-bl)
    kd = jnp.concatenate([blk(kinv, j) * jnp.exp(b_end - r[j]) for j in range(nsub)], axis=0).astype(BF16)
    upd = _dot_tn(v, kd)
    zero = jnp.zeros((SC, HEAD_W), F32)
    parts = []
    for i in range(nsub):
        seen = [j for j in range(nsub) if (j <= i if not rev else j >= i)]
        kk = jnp.concatenate(
            [(blk(kinv, j) if j == i else blk(kinv, j) * jnp.exp(r[i] - r[j])) if j in seen else zero
             for j in range(nsub)], axis=0).astype(BF16)
        parts.append(_dot_nt(blk(qt, i).astype(BF16), kk))
    qb = jnp.concatenate([blk(qt, i) * jnp.exp(r[i]) for i in range(nsub)], axis=0).astype(BF16)
    yield

    st = state[0]
    amat = jnp.where(causal, jnp.concatenate(parts, axis=0), 0.0)
    o_ref[rows, :] = _dot_nt(qb, st.astype(BF16)) + _dot(amat.astype(BF16), v)
    state[0] = st * jnp.exp(b_end) + upd


def _hgrn_kernel(q_ref, v_ref, zf_ref, zb_ref, g_ref, gp_ref, nw_ref, y_ref,
                 of_ref, ob_ref, sf_ref, sb_ref, *, n_ctx, n_tok):
    L = HGRN_CHUNK
    nc, ncc = n_tok // L, n_ctx // L
    U = _unroll(nc, HGRN_UNROLL)
    sf_ref[...] = jnp.zeros_like(sf_ref)
    sb_ref[...] = jnp.zeros_like(sb_ref)

    def body(it, carry):
        sf = [sf_ref[...]]
        sb = [sb_ref[...]]
        gens = []
        for u in range(U):
            step = it * U + u
            gens.append(_hgrn_chunk(q_ref, v_ref, zf_ref, gp_ref.at[0:3], sf, of_ref,
                                    pl.multiple_of(step * L, L), False))
            cb = jnp.where(step < ncc, ncc - 1 - step, nc - 1 - (step - ncc))
            gens.append(_hgrn_chunk(q_ref, v_ref, zb_ref, gp_ref.at[3:6], sb, ob_ref,
                                    pl.multiple_of(cb * L, L), True))
        _round_robin(gens)
        sf_ref[...] = sf[0]
        sb_ref[...] = sb[0]
        return carry

    lax.fori_loop(0, nc // U, body, 0)

    fr = 256
    def fin(i, carry):
        rows = pl.ds(pl.multiple_of(i * fr, fr), fr)
        o = of_ref[rows, :] + ob_ref[rows, :]
        y_ref[rows, :] = (_rms(o) * nw_ref[...] * _silu(g_ref[rows, :])).astype(y_ref.dtype)
        return carry

    lax.fori_loop(0, n_tok // fr, fin, 0)


def _hgrn(p, gate_par, norm_w, l, n_ctx, cast_src=None):
    b, t, _ = p.shape
    nh = GROUP_W // HEAD_W

    def col(seg):
        return pl.BlockSpec((None, t, HEAD_W), lambda bi, h: (bi, 0, seg * nh + h))

    return _mixer_call(
        functools.partial(_hgrn_kernel, n_ctx=n_ctx, n_tok=t),
        (b, HEADS),
        [col(0), col(1), col(2), col(3), col(4),
         pl.BlockSpec((None, 8, HEAD_W), lambda bi, h: (l, 0, h)),
         pl.BlockSpec((None, 1, HEAD_W), lambda bi, h: (l, 0, h))],
        pl.BlockSpec((None, t, HEAD_W), lambda bi, h: (bi, 0, h)),
        jax.ShapeDtypeStruct((b, t, GROUP_W), BF16),
        [pltpu.VMEM((t, HEAD_W), F32), pltpu.VMEM((t, HEAD_W), F32),
         pltpu.VMEM((HEAD_W, HEAD_W), F32), pltpu.VMEM((HEAD_W, HEAD_W), F32)],
        (p, p, p, p, p, gate_par, norm_w), cast_src, l + 1)


def _rope(t, cos, sin_signed):
    lane = lax.broadcasted_iota(jnp.int32, t.shape, 1)
    first = (lane & 31) < 16
    rot = jnp.where(first, pltpu.roll(t, HEAD_W - 16, 1), pltpu.roll(t, 16, 1))
    return t * cos + rot * sin_signed


def _softmax_rows(s):
    e = jnp.exp(s - jnp.max(s, axis=-1, keepdims=True))
    return e * (1.0 / jnp.sum(e, axis=-1, keepdims=True))


def _diff_kernel(q_ref, k_ref, v_ref, cos_ref, sin_ref, lam_ref, nw_ref, y_ref, kb_ref, vb_ref,
                 *, n_ctx, tq, lam_init):
    qi = pl.program_id(2)

    @pl.when(qi == 0)
    def _():
        kb_ref[...] = _rope(k_ref[...], cos_ref[...], sin_ref[...]).astype(BF16)
        v = v_ref[...]
        vb_ref[...] = jnp.concatenate([v, jnp.ones_like(v)], axis=1).astype(BF16)

    lam = lam_ref[...]
    lam_full = (jnp.exp(jnp.sum(lam[0:1] * lam[1:2], axis=-1, keepdims=True))
                - jnp.exp(jnp.sum(lam[2:3] * lam[3:4], axis=-1, keepdims=True)) + lam_init)
    rows = pl.ds(pl.multiple_of(qi * tq, tq), tq)
    q = _rope(q_ref[...], cos_ref[rows, :], sin_ref[rows, :]) * ((HEAD_W // 2) ** -0.5 * LOG2E)
    lane = lax.broadcasted_iota(jnp.int32, q.shape, 1)
    q0 = jnp.where(lane < HEAD_W // 2, q, 0.0).astype(BF16)
    q1 = jnp.where(lane >= HEAD_W // 2, q, 0.0).astype(BF16)

    def attend(kb, vb):
        s0 = _dot_nt(q0, kb)
        s1 = _dot_nt(q1, kb)
        e0 = jnp.exp2(s0 - jnp.max(s0, axis=-1, keepdims=True)).astype(BF16)
        e1 = jnp.exp2(s1 - jnp.max(s1, axis=-1, keepdims=True)).astype(BF16)
        t0 = _dot(e0, vb)
        t1 = _dot(e1, vb)
        o = t0[:, :HEAD_W] / t0[:, HEAD_W:] - lam_full * (t1[:, :HEAD_W] / t1[:, HEAD_W:])
        y_ref[...] = (_rms(o) * nw_ref[...] * (1.0 - lam_init)).astype(y_ref.dtype)

    @pl.when(qi == 0)
    def _():
        attend(kb_ref[0:n_ctx, :], vb_ref[0:n_ctx, :])

    @pl.when(qi > 0)
    def _():
        attend(kb_ref[...], vb_ref[...])


def _diff(p, cos, sin_signed, lam, norm_w, l, n_ctx, lam_init, cast_src=None):
    b, t, _ = p.shape
    tq = n_ctx
    base = 5 * HEADS

    def col(seg, rows, rmap):
        return pl.BlockSpec((None, rows, HEAD_W), lambda bi, h, qi: (bi, rmap(qi), base + seg * HEADS + h))

    return _mixer_call(
        functools.partial(_diff_kernel, n_ctx=n_ctx, tq=tq, lam_init=lam_init),
        (b, HEADS, t // tq),
        [col(0, tq, lambda qi: qi), col(1, t, lambda qi: 0), col(2, t, lambda qi: 0),
         pl.BlockSpec((t, HEAD_W), lambda bi, h, qi: (0, 0)),
         pl.BlockSpec((t, HEAD_W), lambda bi, h, qi: (0, 0)),
         pl.BlockSpec((None, 8, HEAD_W), lambda bi, h, qi: (l, 0, 0)),
         pl.BlockSpec((None, 1, HEAD_W), lambda bi, h, qi: (l, 0, h))],
        pl.BlockSpec((None, tq, HEAD_W), lambda bi, h, qi: (bi, qi, h)),
        jax.ShapeDtypeStruct((b, t, GROUP_W), BF16),
        [pltpu.VMEM((t, HEAD_W), BF16), pltpu.VMEM((t, 2 * HEAD_W), BF16)],
        (p, p, p, cos, sin_signed, lam, norm_w), cast_src, l + 1)


def _na_kernel(q_ref, k_ref, v_ref, bias_ref, y_ref, kb_ref, vb_ref, *, n_ctx, grid_rows, kh):
    scale = HEAD_W ** -0.5
    kb_ref[...] = k_ref[...].astype(BF16)
    vb_ref[...] = v_ref[...].astype(BF16)
    kc = kb_ref[0:n_ctx, :]
    vc = vb_ref[0:n_ctx, :]

    qc = (q_ref[0:n_ctx, :] * scale).astype(BF16)
    pc = _softmax_rows(_dot_nt(qc, kc))
    y_ref[0:n_ctx, :] = _dot(pc.astype(BF16), vc).astype(y_ref.dtype)

    win = kh * GRID_W

    def one_row(r, out):
        rows = pl.ds(pl.multiple_of(n_ctx + r * GRID_W, GRID_W), GRID_W)
        q = (q_ref[rows, :] * scale).astype(BF16)
        start = jnp.clip(r - kh // 2, 0, grid_rows - kh)
        wrows = pl.ds(pl.multiple_of(n_ctx + start * GRID_W, GRID_W), win)
        sb = _dot_nt(q, kb_ref[wrows, :]) + bias_ref[r - start]
        sc = _dot_nt(q, kc)
        yield
        m = jnp.maximum(jnp.max(sb, axis=-1, keepdims=True), jnp.max(sc, axis=-1, keepdims=True))
        eb = jnp.exp(sb - m)
        ec = jnp.exp(sc - m)
        inv = 1.0 / (jnp.sum(eb, axis=-1, keepdims=True) + jnp.sum(ec, axis=-1, keepdims=True))
        pb = (eb * inv).astype(BF16)
        pc_ = (ec * inv).astype(BF16)
        yield
        out.append(_dot(pb, vb_ref[wrows, :]) + _dot(pc_, vc))

    nu = _unroll(grid_rows, NA_UNROLL)

    def body(it, carry):
        outs = [[] for _ in range(nu)]
        _round_robin([one_row(it * nu + u, outs[u]) for u in range(nu)])
        o = jnp.concatenate([x[0] for x in outs], axis=0)
        rows = pl.ds(pl.multiple_of(n_ctx + it * (nu * GRID_W), GRID_W), nu * GRID_W)
        y_ref[rows, :] = o.astype(y_ref.dtype)
        return carry

    lax.fori_loop(0, grid_rows // nu, body, 0)


def _na(p, bias, l, n_ctx, cast_src=None):
    b, t, _ = p.shape
    grid_rows = (t - n_ctx) // GRID_W
    kh = min(NA_KH, grid_rows)
    base = 8 * HEADS

    def col(seg):
        return pl.BlockSpec((None, t, HEAD_W), lambda bi, h: (bi, 0, base + seg * HEADS + h))

    return _mixer_call(
        functools.partial(_na_kernel, n_ctx=n_ctx, grid_rows=grid_rows, kh=kh),
        (b, HEADS),
        [col(0), col(1), col(2),
         pl.BlockSpec((None, None, kh, GRID_W, kh * GRID_W), lambda bi, h: (l, h, 0, 0, 0))],
        pl.BlockSpec((None, t, HEAD_W), lambda bi, h: (bi, 0, h)),
        jax.ShapeDtypeStruct((b, t, GROUP_W), BF16),
        [pltpu.VMEM((t, HEAD_W), BF16), pltpu.VMEM((t, HEAD_W), BF16)],
        (p, p, p, bias), cast_src, l + 1)


def _mlstm_chunk(q_ref, k_ref, v_ref, gr_ref, fb, state, h_ref, r0, rev):
    L = MLSTM_CHUNK
    rows = pl.ds(r0, L)
    gi, gf = (2, 3) if rev else (0, 1)
    q = (q_ref[rows, :] * (HEAD_W ** -0.5)).astype(BF16)
    k = k_ref[rows, :]
    v = v_ref[rows, :]
    vaug = jnp.concatenate([v, jnp.ones_like(v)], axis=1).astype(BF16)
    grow = gr_ref[:, rows]
    i_row = grow[gi:gi + 1, :]
    f_rows = _log_sigmoid(grow + fb)

    t_i = lax.broadcasted_iota(jnp.int32, (L, L), 0)
    s_i = lax.broadcasted_iota(jnp.int32, (L, L), 1)
    causal = (s_i <= t_i) if not rev else (s_i >= t_i)
    b_col = jnp.sum(jnp.where(causal, f_rows[gf:gf + 1, :], 0.0), axis=1, keepdims=True)
    tri_t = jnp.where((t_i <= s_i) if not rev else (t_i >= s_i), 1.0, 0.0).astype(BF16)
    fr3 = _split3(f_rows)
    b_row = (_dot(fr3[0], tri_t) + _dot(fr3[1], tri_t) + _dot(fr3[2], tri_t))[gf:gf + 1, :]
    b_tot = b_row[:, L - 1:L] if not rev else b_row[:, 0:1]
    log_w = jnp.where(causal, b_col - b_row + i_row, NEG)
    row_max = jnp.max(log_w, axis=1, keepdims=True)
    upd_row = b_tot - b_row + i_row
    upd_max = jnp.max(upd_row, axis=1, keepdims=True)
    s_qk = _dot_nt(q, k.astype(BF16))
    k_t = k.T
    yield

    m = state[1]
    inter = b_col + m
    m_t = jnp.maximum(inter, row_max)
    w_inter = jnp.exp(inter - m_t)
    w_intra = (jnp.exp(log_w - m_t) * s_qk).astype(BF16)
    m_new = jnp.maximum(b_tot + m, upd_max)
    w_s = jnp.exp(b_tot + m - m_new)
    state[1] = m_new
    upd = _dot((k_t * jnp.exp(upd_row - m_new)).astype(BF16), vaug)
    intra = _dot(w_intra, vaug)
    yield

    c_aug = state[0]
    tot = w_inter * _dot(q, c_aug.astype(BF16)) + intra
    state[0] = w_s * c_aug + upd
    h_ref[rows, :] = tot[:, :HEAD_W] / jnp.maximum(jnp.abs(tot[:, HEAD_W:]), jnp.exp(-m_t))


def _mlstm_kernel(fb_ref, q_ref, k_ref, v_ref, og_ref, gr_ref, nw_ref, y_ref,
                  hf_ref, hb_ref, cf_ref, cb_ref, mf_ref, mb_ref, *, n_ctx, n_tok):
    L = MLSTM_CHUNK
    nc, ncc = n_tok // L, n_ctx // L
    U = _unroll(nc, MLSTM_UNROLL)
    h = pl.program_id(1)
    fb_f = fb_ref[0, h]
    fb_b = fb_ref[1, h]
    for ref in (cf_ref, cb_ref, mf_ref, mb_ref):
        ref[...] = jnp.zeros_like(ref)

    def body(it, carry):
        sf = [cf_ref[...], mf_ref[...]]
        sb = [cb_ref[...], mb_ref[...]]
        gens = []
        for u in range(U):
            step = it * U + u
            gens.append(_mlstm_chunk(q_ref, k_ref, v_ref, gr_ref, fb_f, sf, hf_ref,
                                     pl.multiple_of(step * L, L), False))
            ck = jnp.where(step < ncc, ncc - 1 - step, nc - 1 - (step - ncc))
            gens.append(_mlstm_chunk(q_ref, k_ref, v_ref, gr_ref, fb_b, sb, hb_ref,
                                     pl.multiple_of(ck * L, L), True))
        _round_robin(gens)
        cf_ref[...], mf_ref[...] = sf
        cb_ref[...], mb_ref[...] = sb
        return carry

    lax.fori_loop(0, nc // U, body, 0)

    fr = 256
    def fin(i, carry):
        rows = pl.ds(pl.multiple_of(i * fr, fr), fr)
        o = jax.nn.sigmoid(og_ref[rows, :]) * (hf_ref[rows, :] + hb_ref[rows, :])
        y_ref[rows, :] = (_rms(o) * nw_ref[...]).astype(y_ref.dtype)
        return carry

    lax.fori_loop(0, n_tok // fr, fin, 0)


def _mlstm(p, grow, f_bias, norm_w, l, n_ctx, cast_src=None):
    b, t, _ = p.shape
    base = 11 * HEADS

    def col(seg):
        return pl.BlockSpec((None, t, HEAD_W), lambda bi, h: (bi, 0, base + seg * HEADS + h))

    return _mixer_call(
        functools.partial(_mlstm_kernel, n_ctx=n_ctx, n_tok=t),
        (b, HEADS),
        [pl.BlockSpec(memory_space=pltpu.SMEM),
         col(0), col(1), col(2), col(3),
         pl.BlockSpec((None, None, 8, t), lambda bi, h: (bi, h, 0, 0)),
         pl.BlockSpec((None, 1, HEAD_W), lambda bi, h: (l, 0, h))],
        pl.BlockSpec((None, t, HEAD_W), lambda bi, h: (bi, 0, h)),
        jax.ShapeDtypeStruct((b, t, GROUP_W), BF16),
        [pltpu.VMEM((t, HEAD_W), F32), pltpu.VMEM((t, HEAD_W), F32),
         pltpu.VMEM((HEAD_W, 2 * HEAD_W), F32), pltpu.VMEM((HEAD_W, 2 * HEAD_W), F32),
         pltpu.VMEM((1, 1), F32), pltpu.VMEM((1, 1), F32)],
        (f_bias, p, p, p, p, grow, norm_w), cast_src, l + 1)


def _outproj_kernel(ya_ref, yb_ref, yc_ref, yd_ref, w_ref, x_ref, mx_ref, mc_ref, o_ref, *, n_ctx, tm):
    i = pl.program_id(1)
    acc = _dot(ya_ref[...], w_ref[0])
    acc += _dot(yb_ref[...], w_ref[1])
    acc += _dot(yc_ref[...], w_ref[2])
    acc += _dot(yd_ref[...], w_ref[3])
    rows = i * tm + lax.broadcasted_iota(jnp.int32, (tm, 1), 0)
    gate = jnp.where(rows < n_ctx, mc_ref[2:3, :], mx_ref[2:3, :])
    o_ref[...] = x_ref[...] + gate * acc


def _outproj(ys, w_out, xs, mods, l, n_ctx):
    b, t, d = xs.shape
    tm, tn = ROW_TILE, 1024
    nb = mods.shape[1] - 1
    yspec = pl.BlockSpec((None, tm, GROUP_W), lambda bi, i, j: (bi, i, 0))
    return pl.pallas_call(
        functools.partial(_outproj_kernel, n_ctx=n_ctx, tm=tm),
        grid=(b, t // tm, d // tn),
        in_specs=[yspec, yspec, yspec, yspec,
                  pl.BlockSpec((4, GROUP_W, tn), lambda bi, i, j: (0, 0, j)),
                  pl.BlockSpec((None, tm, tn), lambda bi, i, j: (bi, i, j)),
                  pl.BlockSpec((None, None, 6, tn), lambda bi, i, j: (l, bi, 0, j)),
                  pl.BlockSpec((None, None, 6, tn), lambda bi, i, j: (l, nb, 0, j))],
        out_specs=pl.BlockSpec((None, tm, tn), lambda bi, i, j: (bi, i, j)),
        out_shape=jax.ShapeDtypeStruct((b, t, d), F32),
        compiler_params=_cparams("arbitrary", "arbitrary", "arbitrary"),
    )(*ys, w_out, xs, mods, mods)


def _ffn_kernel(xm_ref, xp_ref, xn_ref, mx_ref, mc_ref, wa_ref, wg_ref, cwa_ref, cwg_ref,
                cba_ref, cbg_ref, wd_ref, o_ref, hs_ref, ua_ref, ug_ref, *, n_ctx, n_tok, tm):
    assert n_ctx % FFN_SUB == 0 and n_tok % FFN_SUB == 0 and tm % FFN_SUB == 0
    i = pl.program_id(1)
    f = pl.program_id(2)
    nf = pl.num_programs(2) - 1
    row0 = i * tm
    cur = f % 2
    prev = 1 - cur

    def up_project():
        ua_ref[cur] = _dot(hs_ref[...], wa_ref[...])
        ug_ref[cur] = _dot(hs_ref[...], wg_ref[...])

    def finish():
        ua = ua_ref.at[prev]
        ug = ug_ref.at[prev]
        for r in range(0, tm, FFN_SUB):
            g0 = row0 + r
            no_prev = (g0 == 0) | (g0 == n_ctx)
            no_next = (g0 + FFN_SUB == n_ctx) | (g0 + FFN_SUB == n_tok)
            tile_row = lax.broadcasted_iota(jnp.int32, (8, 1), 0)
            drop_first = (tile_row == 0) & no_prev
            drop_last = (tile_row == 7) & no_next

            def conv(u_ref, cw_ref, cb_ref):
                lo = HALO + r
                before = u_ref[lo - 1:lo - 1 + FFN_SUB, :]
                after = u_ref[lo + 1:lo + 1 + FFN_SUB, :]
                before = jnp.concatenate([jnp.where(drop_first, 0.0, before[:8]), before[8:]], axis=0)
                after = jnp.concatenate([after[:-8], jnp.where(drop_last, 0.0, after[-8:])], axis=0)
                return (before * cw_ref[0:1, :] + u_ref[lo:lo + FFN_SUB, :] * cw_ref[1:2, :]
                        + after * cw_ref[2:3, :] + cb_ref[...])

            act = (_silu(conv(ua, cwa_ref, cba_ref)) * conv(ug, cwg_ref, cbg_ref)).astype(BF16)
            o_ref[r:r + FFN_SUB, :] += _dot(act, wd_ref[...])

    @pl.when(f == 0)
    def _():
        hs_ref[0:HALO, :] = _modulate(xp_ref[...], mx_ref, mc_ref, row0 - HALO, 3, n_ctx).astype(BF16)
        _modulate_into(hs_ref, HALO, xm_ref, mx_ref, mc_ref, row0, 3, n_ctx)
        hs_ref[HALO + tm:, :] = _modulate(xn_ref[...], mx_ref, mc_ref, row0 + tm, 3, n_ctx).astype(BF16)
        o_ref[...] = jnp.zeros_like(o_ref)
        up_project()

    @pl.when((f > 0) & (f < nf))
    def _():
        up_project()
        finish()

    @pl.when(f == nf)
    def _():
        finish()
        rows = row0 + lax.broadcasted_iota(jnp.int32, (tm, 1), 0)
        gate = jnp.where(rows < n_ctx, mc_ref[5:6, :], mx_ref[5:6, :])
        o_ref[...] = xm_ref[...] + gate * o_ref[...]


def _ffn(xs, mods, w_up, conv_w, conv_b, w_down, l, n_ctx):
    b, t, d = xs.shape
    tm, tf = FFN_ROW_TILE, FF_TILE
    d_ff = w_down.shape[0]
    nf = d_ff // tf
    nb = mods.shape[1] - 1
    hb = tm // HALO
    last = t // HALO - 1

    def up(f):
        return jnp.minimum(f, nf - 1)

    def fin(f):
        return jnp.maximum(f - 1, 0)

    return pl.pallas_call(
        functools.partial(_ffn_kernel, n_ctx=n_ctx, n_tok=t, tm=tm),
        grid=(b, t // tm, nf + 1),
        in_specs=[
            pl.BlockSpec((None, tm, d), lambda bi, i, f: (bi, i, 0)),
            pl.BlockSpec((None, HALO, d), lambda bi, i, f: (bi, jnp.maximum(i * hb - 1, 0), 0)),
            pl.BlockSpec((None, HALO, d), lambda bi, i, f: (bi, jnp.minimum((i + 1) * hb, last), 0)),
            pl.BlockSpec((None, None, 6, d), lambda bi, i, f: (l, bi, 0, 0)),
            pl.BlockSpec((None, None, 6, d), lambda bi, i, f: (l, nb, 0, 0)),
            pl.BlockSpec((d, tf), lambda bi, i, f: (0, up(f))),
            pl.BlockSpec((d, tf), lambda bi, i, f: (0, nf + up(f))),
            pl.BlockSpec((None, CONV_W, tf), lambda bi, i, f: (l, 0, fin(f))),
            pl.BlockSpec((None, CONV_W, tf), lambda bi, i, f: (l, 0, nf + fin(f))),
            pl.BlockSpec((None, 1, tf), lambda bi, i, f: (l, 0, fin(f))),
            pl.BlockSpec((None, 1, tf), lambda bi, i, f: (l, 0, nf + fin(f))),
            pl.BlockSpec((tf, d), lambda bi, i, f: (fin(f), 0)),
        ],
        out_specs=pl.BlockSpec((None, tm, d), lambda bi, i, f: (bi, i, 0)),
        out_shape=jax.ShapeDtypeStruct((b, t, d), F32),
        scratch_shapes=[pltpu.VMEM((tm + 2 * HALO, d), BF16),
                        pltpu.VMEM((2, tm + 2 * HALO, tf), F32),
                        pltpu.VMEM((2, tm + 2 * HALO, tf), F32)],
        compiler_params=_cparams("arbitrary", "arbitrary", "arbitrary"),
    )(xs, xs, xs, mods, mods, w_up, w_up, conv_w, conv_w, conv_b, conv_b, w_down)


def _final_kernel(x_ref, w_ref, o_ref):
    o_ref[...] = _rms(x_ref[...]) * w_ref[...]


def _final_norm(xs, w, n_ctx):
    b, t, d = xs.shape
    tr = n_ctx
    off = n_ctx // tr
    return pl.pallas_call(
        _final_kernel,
        grid=(b, (t - n_ctx) // tr),
        in_specs=[pl.BlockSpec((None, tr, d), lambda bi, i: (bi, i + off, 0)),
                  pl.BlockSpec((1, d), lambda bi, i: (0, 0))],
        out_specs=pl.BlockSpec((None, tr, d), lambda bi, i: (bi, i, 0)),
        out_shape=jax.ShapeDtypeStruct((b, t - n_ctx, d), F32),
        compiler_params=_cparams("arbitrary", "arbitrary"),
    )(xs, w.reshape(1, d))


def _lambda_init(layer_idx):
    return 0.8 - 0.6 * math.exp(-0.3 * layer_idx)


def _rope_tables(n_ctx, n_lat):
    dh = HEAD_W // 2
    t = np.arange(n_lat)
    row = (t // GRID_W).astype(np.float32)
    colp = (t % GRID_W).astype(np.float32)
    half = dh // 2
    inv = (ROPE_THETA ** (-np.arange(0, half, 2, dtype=np.float32) / half)).astype(np.float32)
    ar = row[:, None] * inv
    ac = colp[:, None] * inv
    ang = jnp.asarray(np.concatenate([ar, ar, ac, ac], axis=-1))
    cos, sin = jnp.cos(ang), jnp.sin(ang)
    sign = np.where((np.arange(dh) % 32) < 16, -1.0, 1.0).astype(np.float32)
    cos = jnp.concatenate([jnp.ones((n_ctx, dh), F32), cos], axis=0)
    sin = jnp.concatenate([jnp.zeros((n_ctx, dh), F32), sin * sign], axis=0)
    return jnp.tile(cos, (1, 2)), jnp.tile(sin, (1, 2))


def _na_bias(na_rpb, kh):
    w = np.arange(GRID_W)
    col_start = np.clip(w - NA_KW // 2, 0, GRID_W - NA_KW)
    col_ok = (w[None, :] >= col_start[:, None]) & (w[None, :] < col_start[:, None] + NA_KW)
    coff = np.clip(w[None, :] - w[:, None], -(NA_KW - 1), NA_KW - 1) + (NA_KW - 1)
    onehot = jnp.asarray(coff[:, :, None] == np.arange(2 * NA_KW - 1), F32)
    toe = jnp.einsum('dhrc,qkc->dhrqk', na_rpb.astype(F32), onehot, precision=lax.Precision.HIGHEST)
    toe = jnp.where(col_ok, toe, NEG).transpose(0, 1, 3, 2, 4)
    shape = (na_rpb.shape[0], na_rpb.shape[1], GRID_W, kh * GRID_W)
    per_id = [toe[:, :, :, NA_KH - 1 - i:NA_KH - 1 - i + kh].reshape(shape) for i in range(kh)]
    return jnp.stack(per_id, axis=2)


def kernel(x, c, ctx, c_ctx, w_ada, b_ada, w_in, hgrn_lb, hgrn_norm_w, diff_lam, diff_norm_w, na_rpb,
           mlstm_f_bias, mlstm_norm_w, w_out, w_up, conv_w, conv_b, w_down, final_norm_w):
    bsz, n_lat, d = x.shape
    n_ctx = ctx.shape[1]
    depth = w_in.shape[0]
    d_ff = w_down.shape[1]
    grid_rows = n_lat // GRID_W

    xs = jnp.concatenate([ctx, x], axis=1)
    n_mod = -(-(bsz + 1) // 8) * 8
    cc = jnp.zeros((n_mod, d), F32).at[:bsz].set(c).at[n_mod - 1].set(c_ctx)
    mods = _ada(cc, w_ada, b_ada).reshape(depth, n_mod, 6, d)

    w_in_all = w_in.astype(BF16)
    w_out_b, w_up_b, w_down_b = (w[0].astype(BF16) for w in (w_out, w_up, w_down))
    w_gate_b = jnp.pad(w_in[:, :, D_MAIN:], ((0, 0), (0, 0), (0, HEAD_W - N_GATES))).astype(BF16)
    conv_b3 = conv_b.reshape(depth, 1, 2 * d_ff)

    lb = jnp.cumsum(jax.nn.softmax(hgrn_lb.astype(F32), axis=0), axis=0)
    lb = lb - lb[:1]
    zero = jnp.zeros_like(lb[:, :1])
    gate_par = jnp.concatenate([jnp.log(lb[:, 0:1]), jnp.log1p(-lb[:, 0:1]), 1.0 - lb[:, 0:1],
                                jnp.log(lb[:, 1:2]), jnp.log1p(-lb[:, 1:2]), 1.0 - lb[:, 1:2],
                                zero, zero], axis=1)
    cos, sin_signed = _rope_tables(n_ctx, n_lat)
    lam = jnp.pad(diff_lam.astype(F32), ((0, 0), (0, 4), (0, HEAD_W - diff_lam.shape[2])))
    bias = _na_bias(na_rpb, min(NA_KH, grid_rows))
    f_bias = mlstm_f_bias.astype(F32)

    def nw(w):
        return w.astype(F32).reshape(depth, 1, GROUP_W)

    hgrn_nw, diff_nw, mlstm_nw = nw(hgrn_norm_w), nw(diff_norm_w), nw(mlstm_norm_w)

    for l in range(depth):
        p, gates = _inproj(xs, mods, w_in_all, w_gate_b, l, n_ctx)
        g4 = gates[:, :, :N_GATES].reshape(bsz, -1, 4, HEADS)
        grow = jnp.pad(g4.transpose(0, 3, 2, 1), ((0, 0), (0, 0), (0, 4), (0, 0)))
        nxt = l + 1 < depth
        ya, w_up_n = _hgrn(p, gate_par, hgrn_nw, l, n_ctx, w_up if nxt else None)
        yb, _ = _diff(p, cos, sin_signed, lam, diff_nw, l, n_ctx, _lambda_init(l))
        yc, w_down_n = _na(p, bias, l, n_ctx, w_down if nxt else None)
        yd, w_out_n = _mlstm(p, grow, f_bias[l], mlstm_nw, l, n_ctx, w_out if nxt else None)
        xs = _outproj((ya, yb, yc, yd), w_out_b.reshape(4, GROUP_W, d), xs, mods, l, n_ctx)
        xs = _ffn(xs, mods, w_up_b, conv_w, conv_b3, w_down_b, l, n_ctx)
        w_out_b, w_up_b, w_down_b = w_out_n, w_up_n, w_down_n
    return _final_norm(xs, final_norm_w, n_ctx)
```

```python
import functools
import math

import numpy as np
import jax
import jax.numpy as jnp
from jax import lax
from jax.experimental import pallas as pl
from jax.experimental.pallas import tpu as pltpu

F32 = jnp.float32
BF16 = jnp.bfloat16

HEADS = 4
HEAD_W = 128
GROUP_W = HEADS * HEAD_W
GRID_W = 64
NA_KH = 8
NA_KW = 16
ROPE_THETA = 10000.0
EPS = 1e-6
CONV_W = 3
D_MAIN = 15 * GROUP_W
N_GATES = 4 * HEADS

VMEM_LIMIT_BYTES = 56 * 1024 * 1024
NEG = -1e30
LOG2E = 1.4426950408889634

HGRN_CHUNK = 64
HGRN_SUB = 16
HGRN_UNROLL = 4
MLSTM_CHUNK = 128
MLSTM_UNROLL = 3
DIFF_Q_TILE = 768
NA_UNROLL = 16
ROW_TILE = 768
INPROJ_TILE = 1280
FF_TILE = 512
HALO = 16
FFN_ROW_TILE = 768
FFN_SUB = 256
MOD_SUB = 256


def _unroll(n, target):
    return max(u for u in range(1, target + 1) if n % u == 0)


def _cparams(*sem):
    return pltpu.CompilerParams(dimension_semantics=sem, vmem_limit_bytes=VMEM_LIMIT_BYTES)


def _dot(a, b):
    return jnp.dot(a, b, preferred_element_type=F32)


def _dot_nt(a, b):
    return lax.dot_general(a, b, (((1,), (1,)), ((), ())), preferred_element_type=F32)


def _dot_tn(a, b):
    return lax.dot_general(a, b, (((0,), (0,)), ((), ())), preferred_element_type=F32)


def _silu(t):
    return t * jax.nn.sigmoid(t)


def _log_sigmoid(z):
    return jnp.minimum(z, 0.0) - jnp.log1p(jnp.exp(-jnp.abs(z)))


def _rms(t):
    return t * lax.rsqrt(jnp.mean(t * t, axis=-1, keepdims=True) + EPS)


def _split3(t):
    hi = t.astype(BF16)
    r1 = t - hi.astype(F32)
    mid = r1.astype(BF16)
    lo = (r1 - mid.astype(F32)).astype(BF16)
    return hi, mid, lo


def _modulate(x, mx_ref, mc_ref, row0, k, n_ctx):
    rows = row0 + lax.broadcasted_iota(jnp.int32, (x.shape[0], 1), 0)
    is_ctx = rows < n_ctx
    shift = jnp.where(is_ctx, mc_ref[k:k + 1, :], mx_ref[k:k + 1, :])
    scale = jnp.where(is_ctx, mc_ref[k + 1:k + 2, :], mx_ref[k + 1:k + 2, :])
    return _rms(x) * (1.0 + scale) + shift


def _modulate_into(dst_ref, dst_off, x_ref, mx_ref, mc_ref, row0, k, n_ctx):
    n = x_ref.shape[0]
    sub = math.gcd(n, MOD_SUB)
    for r in range(0, n, sub):
        h = _modulate(x_ref[r:r + sub, :], mx_ref, mc_ref, row0 + r, k, n_ctx)
        dst_ref[dst_off + r:dst_off + r + sub, :] = h.astype(dst_ref.dtype)


def _ada_kernel(c_ref, w_ref, b_ref, o_ref):
    s = _silu(c_ref[...]).astype(BF16)
    o_ref[...] = _dot(s, w_ref[...].astype(BF16)) + b_ref[...]


def _ada(cc, w_ada, b_ada):
    depth, d, n = w_ada.shape
    tn = 1024
    return pl.pallas_call(
        _ada_kernel,
        grid=(depth, n // tn),
        in_specs=[
            pl.BlockSpec((8, d), lambda l, j: (0, 0)),
            pl.BlockSpec((None, d, tn), lambda l, j: (l, 0, j)),
            pl.BlockSpec((None, 1, tn), lambda l, j: (l, 0, j)),
        ],
        out_specs=pl.BlockSpec((None, 8, tn), lambda l, j: (l, 0, j)),
        out_shape=jax.ShapeDtypeStruct((depth, 8, n), F32),
        compiler_params=_cparams("arbitrary", "arbitrary"),
    )(cc, w_ada, b_ada.reshape(depth, 1, n))


def _inproj_kernel(x_ref, mx_ref, mc_ref, w_ref, wg_ref, p_ref, gate_ref, h_ref, *, n_ctx, tm):
    i = pl.program_id(1)
    j = pl.program_id(2)

    @pl.when(j == 0)
    def _():
        _modulate_into(h_ref, 0, x_ref, mx_ref, mc_ref, i * tm, 0, n_ctx)
        gate_ref[...] = _dot(h_ref[...], wg_ref[...])

    p_ref[...] = _dot(h_ref[...], w_ref[...])


def _inproj(xs, mods, w_in, w_gate, l, n_ctx):
    b, t, d = xs.shape
    tm, tn = ROW_TILE, INPROJ_TILE
    nb = mods.shape[1] - 1
    return pl.pallas_call(
        functools.partial(_inproj_kernel, n_ctx=n_ctx, tm=tm),
        grid=(b, t // tm, D_MAIN // tn),
        in_specs=[
            pl.BlockSpec((None, tm, d), lambda bi, i, j: (bi, i, 0)),
            pl.BlockSpec((None, None, 6, d), lambda bi, i, j: (l, bi, 0, 0)),
            pl.BlockSpec((None, None, 6, d), lambda bi, i, j: (l, nb, 0, 0)),
            pl.BlockSpec((None, d, tn), lambda bi, i, j: (l, 0, j)),
            pl.BlockSpec((None, d, HEAD_W), lambda bi, i, j: (l, 0, 0)),
        ],
        out_specs=[
            pl.BlockSpec((None, tm, tn), lambda bi, i, j: (bi, i, j)),
            pl.BlockSpec((None, tm, HEAD_W), lambda bi, i, j: (bi, i, 0)),
        ],
        out_shape=[
            jax.ShapeDtypeStruct((b, t, D_MAIN), F32),
            jax.ShapeDtypeStruct((b, t, HEAD_W), F32),
        ],
        scratch_shapes=[pltpu.VMEM((tm, d), BF16)],
        compiler_params=_cparams("arbitrary", "arbitrary", "arbitrary"),
    )(xs, mods, mods, w_in, w_gate)


def _mixer_call(kernel_fn, grid, in_specs, out_spec, out_shape, scratch_shapes, operands,
                cast_src=None, cast_layer=0):
    sem = ("arbitrary",) * len(grid)
    if cast_src is None:
        y = pl.pallas_call(kernel_fn, grid=grid, in_specs=in_specs, out_specs=out_spec, out_shape=out_shape,
                           scratch_shapes=scratch_shapes, compiler_params=_cparams(*sem))(*operands)
        return y, None
    n_in = len(in_specs)
    nblk = grid[0] * grid[1]
    rows, cols = cast_src.shape[1:]
    assert rows % (nblk * 16) == 0, (rows, nblk)
    rb = rows // nblk
    src_spec = pl.BlockSpec((None, rb, cols), lambda bi, h, *r: (cast_layer, bi * grid[1] + h, 0))
    dst_spec = pl.BlockSpec((rb, cols), lambda bi, h, *r: (bi * grid[1] + h, 0))

    def kern(*refs):
        src_ref, y_ref, dst_ref = refs[n_in], refs[n_in + 1], refs[n_in + 2]

        def convert():
            dst_ref[...] = src_ref[...].astype(dst_ref.dtype)

        if len(grid) > 2:
            pl.when(pl.program_id(2) == 0)(convert)
        else:
            convert()
        kernel_fn(*refs[:n_in], y_ref, *refs[n_in + 3:])

    return pl.pallas_call(
        kern, grid=grid, in_specs=list(in_specs) + [src_spec], out_specs=[out_spec, dst_spec],
        out_shape=[out_shape, jax.ShapeDtypeStruct((rows, cols), BF16)],
        scratch_shapes=scratch_shapes, compiler_params=_cparams(*sem))(*operands, cast_src)


def _round_robin(gens):
    live = list(gens)
    while live:
        nxt = []
        for g in live:
            try:
                next(g)
                nxt.append(g)
            except StopIteration:
                pass
        live = nxt


def _hgrn_chunk(q_ref, v_ref, z_ref, gp_ref, state, o_ref, r0, rev):
    L, SC = HGRN_CHUNK, HGRN_SUB
    nsub = L // SC
    rows = pl.ds(r0, L)
    q = _silu(q_ref[rows, :]) * (HEAD_W ** -0.5)
    z = z_ref[rows, :]
    v = v_ref[rows, :].astype(BF16)
    e = jnp.exp(-jnp.abs(z))
    inv = 1.0 / (1.0 + e)
    a = gp_ref[1:2, :] + (jnp.minimum(z, 0.0) - jnp.log(1.0 + e))
    ll = gp_ref[0:1, :]
    lf = jnp.maximum(ll, a) + jnp.log(1.0 + jnp.exp(-jnp.abs(ll - a)))
    k = gp_ref[2:3, :] * (jnp.where(z >= 0.0, e, 1.0) * inv)

    t_i = lax.broadcasted_iota(jnp.int32, (L, L), 0)
    s_i = lax.broadcasted_iota(jnp.int32, (L, L), 1)
    t_blk = t_i & ~(SC - 1)
    if not rev:
        local = (s_i <= t_i) & (s_i >= t_blk)
        causal = s_i <= t_i
    else:
        local = (s_i >= t_i) & (s_i < t_blk + SC)
        causal = s_i >= t_i
    sel = jnp.where(local, 1.0, 0.0).astype(BF16)
    hi, mid, lo = _split3(lf)
    bl = _dot(sel, hi) + _dot(sel, mid) + _dot(sel, lo)
    yield

    def blk(x, i):
        return x[i * SC:(i + 1) * SC]

    order = list(range(nsub)) if not rev else list(range(nsub - 1, -1, -1))
    r = [None] * nsub
    acc = jnp.zeros((1, HEAD_W), F32)
    for i in order:
        r[i] = acc
        acc = acc + (bl[(i + 1) * SC - 1:(i + 1) * SC] if not rev else bl[i * SC:i * SC + 1])
    b_end = acc

    qt = q * jnp.exp(bl)
    kinv = k * jnp.exp(-bl)
    kd = jnp.concatenate([blk(kinv, j) * jnp.exp(b_end - r[j]) for j in range(nsub)], axis=0).astype(BF16)
    upd = _dot_tn(v, kd)
    zero = jnp.zeros((SC, HEAD_W), F32)
    parts = []
    for i in range(nsub):
        seen = [j for j in range(nsub) if (j <= i if not rev else j >= i)]
        kk = jnp.concatenate(
            [(blk(kinv, j) if j == i else blk(kinv, j) * jnp.exp(r[i] - r[j])) if j in seen else zero
             for j in range(nsub)], axis=0).astype(BF16)
        parts.append(_dot_nt(blk(qt, i).astype(BF16), kk))
    qb = jnp.concatenate([blk(qt, i) * jnp.exp(r[i]) for i in range(nsub)], axis=0).astype(BF16)
    yield

    st = state[0]
    amat = jnp.where(causal, jnp.concatenate(parts, axis=0), 0.0)
    o_ref[rows, :] = _dot_nt(qb, st.astype(BF16)) + _dot(amat.astype(BF16), v)
    state[0] = st * jnp.exp(b_end) + upd


def _hgrn_kernel(q_ref, v_ref, zf_ref, zb_ref, g_ref, gp_ref, nw_ref, y_ref,
                 of_ref, ob_ref, sf_ref, sb_ref, *, n_ctx, n_tok):
    L = HGRN_CHUNK
    nc, ncc = n_tok // L, n_ctx // L
    U = _unroll(nc, HGRN_UNROLL)
    sf_ref[...] = jnp.zeros_like(sf_ref)
    sb_ref[...] = jnp.zeros_like(sb_ref)

    def body(it, carry):
        sf = [sf_ref[...]]
        sb = [sb_ref[...]]
        gens = []
        for u in range(U):
            step = it * U + u
            gens.append(_hgrn_chunk(q_ref, v_ref, zf_ref, gp_ref.at[0:3], sf, of_ref,
                                    pl.multiple_of(step * L, L), False))
            cb = jnp.where(step < ncc, ncc - 1 - step, nc - 1 - (step - ncc))
            gens.append(_hgrn_chunk(q_ref, v_ref, zb_ref, gp_ref.at[3:6], sb, ob_ref,
                                    pl.multiple_of(cb * L, L), True))
        _round_robin(gens)
        sf_ref[...] = sf[0]
        sb_ref[...] = sb[0]
        return carry

    lax.fori_loop(0, nc // U, body, 0)

    fr = 256
    def fin(i, carry):
        rows = pl.ds(pl.multiple_of(i * fr, fr), fr)
        o = of_ref[rows, :] + ob_ref[rows, :]
        y_ref[rows, :] = (_rms(o) * nw_ref[...] * _silu(g_ref[rows, :])).astype(y_ref.dtype)
        return carry

    lax.fori_loop(0, n_tok // fr, fin, 0)


def _hgrn(p, gate_par, norm_w, l, n_ctx, cast_src=None):
    b, t, _ = p.shape
    nh = GROUP_W // HEAD_W

    def col(seg):
        return pl.BlockSpec((None, t, HEAD_W), lambda bi, h: (bi, 0, seg * nh + h))

    return _mixer_call(
        functools.partial(_hgrn_kernel, n_ctx=n_ctx, n_tok=t),
        (b, HEADS),
        [col(0), col(1), col(2), col(3), col(4),
         pl.BlockSpec((None, 8, HEAD_W), lambda bi, h: (l, 0, h)),
         pl.BlockSpec((None, 1, HEAD_W), lambda bi, h: (l, 0, h))],
        pl.BlockSpec((None, t, HEAD_W), lambda bi, h: (bi, 0, h)),
        jax.ShapeDtypeStruct((b, t, GROUP_W), BF16),
        [pltpu.VMEM((t, HEAD_W), F32), pltpu.VMEM((t, HEAD_W), F32),
         pltpu.VMEM((HEAD_W, HEAD_W), F32), pltpu.VMEM((HEAD_W, HEAD_W), F32)],
        (p, p, p, p, p, gate_par, norm_w), cast_src, l + 1)


def _rope(t, cos, sin_signed):
    lane = lax.broadcasted_iota(jnp.int32, t.shape, 1)
    first = (lane & 31) < 16
    rot = jnp.where(first, pltpu.roll(t, HEAD_W - 16, 1), pltpu.roll(t, 16, 1))
    return t * cos + rot * sin_signed


def _softmax_rows(s):
    e = jnp.exp(s - jnp.max(s, axis=-1, keepdims=True))
    return e * (1.0 / jnp.sum(e, axis=-1, keepdims=True))


def _diff_kernel(q_ref, k_ref, v_ref, cos_ref, sin_ref, lam_ref, nw_ref, y_ref, kb_ref, vb_ref,
                 *, n_ctx, tq, lam_init):
    qi = pl.program_id(2)

    @pl.when(qi == 0)
    def _():
        kb_ref[...] = _rope(k_ref[...], cos_ref[...], sin_ref[...]).astype(BF16)
        v = v_ref[...]
        vb_ref[...] = jnp.concatenate([v, jnp.ones_like(v)], axis=1).astype(BF16)

    lam = lam_ref[...]
    lam_full = (jnp.exp(jnp.sum(lam[0:1] * lam[1:2], axis=-1, keepdims=True))
                - jnp.exp(jnp.sum(lam[2:3] * lam[3:4], axis=-1, keepdims=True)) + lam_init)
    rows = pl.ds(pl.multiple_of(qi * tq, tq), tq)
    q = _rope(q_ref[...], cos_ref[rows, :], sin_ref[rows, :]) * ((HEAD_W // 2) ** -0.5 * LOG2E)
    lane = lax.broadcasted_iota(jnp.int32, q.shape, 1)
    q0 = jnp.where(lane < HEAD_W // 2, q, 0.0).astype(BF16)
    q1 = jnp.where(lane >= HEAD_W // 2, q, 0.0).astype(BF16)

    def attend(r0, r1, kb, vb):
        s0 = _dot_nt(q0[r0:r1], kb)
        s1 = _dot_nt(q1[r0:r1], kb)
        e0 = jnp.exp2(s0 - jnp.max(s0, axis=-1, keepdims=True)).astype(BF16)
        e1 = jnp.exp2(s1 - jnp.max(s1, axis=-1, keepdims=True)).astype(BF16)
        t0 = _dot(e0, vb)
        t1 = _dot(e1, vb)
        o = t0[:, :HEAD_W] / t0[:, HEAD_W:] - lam_full * (t1[:, :HEAD_W] / t1[:, HEAD_W:])
        y_ref[r0:r1, :] = (_rms(o) * nw_ref[...] * (1.0 - lam_init)).astype(y_ref.dtype)

    @pl.when(qi == 0)
    def _():
        attend(0, n_ctx, kb_ref[0:n_ctx, :], vb_ref[0:n_ctx, :])
        if tq > n_ctx:
            attend(n_ctx, tq, kb_ref[...], vb_ref[...])

    @pl.when(qi > 0)
    def _():
        attend(0, tq, kb_ref[...], vb_ref[...])


def _diff(p, cos, sin_signed, lam, norm_w, l, n_ctx, lam_init, cast_src=None):
    b, t, _ = p.shape
    tq = DIFF_Q_TILE if (t % DIFF_Q_TILE == 0 and DIFF_Q_TILE % n_ctx == 0) else n_ctx
    base = 5 * HEADS

    def col(seg, rows, rmap):
        return pl.BlockSpec((None, rows, HEAD_W), lambda bi, h, qi: (bi, rmap(qi), base + seg * HEADS + h))

    return _mixer_call(
        functools.partial(_diff_kernel, n_ctx=n_ctx, tq=tq, lam_init=lam_init),
        (b, HEADS, t // tq),
        [col(0, tq, lambda qi: qi), col(1, t, lambda qi: 0), col(2, t, lambda qi: 0),
         pl.BlockSpec((t, HEAD_W), lambda bi, h, qi: (0, 0)),
         pl.BlockSpec((t, HEAD_W), lambda bi, h, qi: (0, 0)),
         pl.BlockSpec((None, 8, HEAD_W), lambda bi, h, qi: (l, 0, 0)),
         pl.BlockSpec((None, 1, HEAD_W), lambda bi, h, qi: (l, 0, h))],
        pl.BlockSpec((None, tq, HEAD_W), lambda bi, h, qi: (bi, qi, h)),
        jax.ShapeDtypeStruct((b, t, GROUP_W), BF16),
        [pltpu.VMEM((t, HEAD_W), BF16), pltpu.VMEM((t, 2 * HEAD_W), BF16)],
        (p, p, p, cos, sin_signed, lam, norm_w), cast_src, l + 1)


def _na_kernel(q_ref, k_ref, v_ref, bias_ref, y_ref, kb_ref, vb_ref, *, n_ctx, grid_rows, kh):
    scale = HEAD_W ** -0.5
    kb_ref[...] = k_ref[...].astype(BF16)
    vb_ref[...] = v_ref[...].astype(BF16)
    kc = kb_ref[0:n_ctx, :]
    vc = vb_ref[0:n_ctx, :]

    qc = (q_ref[0:n_ctx, :] * scale).astype(BF16)
    pc = _softmax_rows(_dot_nt(qc, kc))
    y_ref[0:n_ctx, :] = _dot(pc.astype(BF16), vc).astype(y_ref.dtype)

    win = kh * GRID_W

    def one_row(r, out):
        rows = pl.ds(pl.multiple_of(n_ctx + r * GRID_W, GRID_W), GRID_W)
        q = (q_ref[rows, :] * scale).astype(BF16)
        start = jnp.clip(r - kh // 2, 0, grid_rows - kh)
        wrows = pl.ds(pl.multiple_of(n_ctx + start * GRID_W, GRID_W), win)
        sb = _dot_nt(q, kb_ref[wrows, :]) + bias_ref[r - start]
        sc = _dot_nt(q, kc)
        yield
        m = jnp.maximum(jnp.max(sb, axis=-1, keepdims=True), jnp.max(sc, axis=-1, keepdims=True))
        eb = jnp.exp(sb - m)
        ec = jnp.exp(sc - m)
        inv = 1.0 / (jnp.sum(eb, axis=-1, keepdims=True) + jnp.sum(ec, axis=-1, keepdims=True))
        pb = (eb * inv).astype(BF16)
        pc_ = (ec * inv).astype(BF16)
        yield
        out.append(_dot(pb, vb_ref[wrows, :]) + _dot(pc_, vc))

    nu = _unroll(grid_rows, NA_UNROLL)

    def body(it, carry):
        outs = [[] for _ in range(nu)]
        _round_robin([one_row(it * nu + u, outs[u]) for u in range(nu)])
        o = jnp.concatenate([x[0] for x in outs], axis=0)
        rows = pl.ds(pl.multiple_of(n_ctx + it * (nu * GRID_W), GRID_W), nu * GRID_W)
        y_ref[rows, :] = o.astype(y_ref.dtype)
        return carry

    lax.fori_loop(0, grid_rows // nu, body, 0)


def _na(p, bias, l, n_ctx, cast_src=None):
    b, t, _ = p.shape
    grid_rows = (t - n_ctx) // GRID_W
    kh = min(NA_KH, grid_rows)
    base = 8 * HEADS

    def col(seg):
        return pl.BlockSpec((None, t, HEAD_W), lambda bi, h: (bi, 0, base + seg * HEADS + h))

    return _mixer_call(
        functools.partial(_na_kernel, n_ctx=n_ctx, grid_rows=grid_rows, kh=kh),
        (b, HEADS),
        [col(0), col(1), col(2),
         pl.BlockSpec((None, None, kh, GRID_W, kh * GRID_W), lambda bi, h: (l, h, 0, 0, 0))],
        pl.BlockSpec((None, t, HEAD_W), lambda bi, h: (bi, 0, h)),
        jax.ShapeDtypeStruct((b, t, GROUP_W), BF16),
        [pltpu.VMEM((t, HEAD_W), BF16), pltpu.VMEM((t, HEAD_W), BF16)],
        (p, p, p, bias), cast_src, l + 1)


def _mlstm_chunk(q_ref, k_ref, v_ref, gr_ref, fb, state, h_ref, r0, rev):
    L = MLSTM_CHUNK
    rows = pl.ds(r0, L)
    gi, gf = (2, 3) if rev else (0, 1)
    q = (q_ref[rows, :] * (HEAD_W ** -0.5)).astype(BF16)
    k = k_ref[rows, :]
    v = v_ref[rows, :]
    vaug = jnp.concatenate([v, jnp.ones_like(v)], axis=1).astype(BF16)
    grow = gr_ref[:, rows]
    i_row = grow[gi:gi + 1, :]
    f_rows = _log_sigmoid(grow + fb)

    t_i = lax.broadcasted_iota(jnp.int32, (L, L), 0)
    s_i = lax.broadcasted_iota(jnp.int32, (L, L), 1)
    causal = (s_i <= t_i) if not rev else (s_i >= t_i)
    b_col = jnp.sum(jnp.where(causal, f_rows[gf:gf + 1, :], 0.0), axis=1, keepdims=True)
    tri_t = jnp.where((t_i <= s_i) if not rev else (t_i >= s_i), 1.0, 0.0).astype(BF16)
    fr3 = _split3(f_rows)
    b_row = (_dot(fr3[0], tri_t) + _dot(fr3[1], tri_t) + _dot(fr3[2], tri_t))[gf:gf + 1, :]
    b_tot = b_row[:, L - 1:L] if not rev else b_row[:, 0:1]
    log_w = jnp.where(causal, b_col - b_row + i_row, NEG)
    row_max = jnp.max(log_w, axis=1, keepdims=True)
    upd_row = b_tot - b_row + i_row
    upd_max = jnp.max(upd_row, axis=1, keepdims=True)
    s_qk = _dot_nt(q, k.astype(BF16))
    k_t = k.T
    yield

    m = state[1]
    inter = b_col + m
    m_t = jnp.maximum(inter, row_max)
    w_inter = jnp.exp(inter - m_t)
    w_intra = (jnp.exp(log_w - m_t) * s_qk).astype(BF16)
    m_new = jnp.maximum(b_tot + m, upd_max)
    w_s = jnp.exp(b_tot + m - m_new)
    state[1] = m_new
    upd = _dot((k_t * jnp.exp(upd_row - m_new)).astype(BF16), vaug)
    intra = _dot(w_intra, vaug)
    yield

    c_aug = state[0]
    tot = w_inter * _dot(q, c_aug.astype(BF16)) + intra
    state[0] = w_s * c_aug + upd
    h_ref[rows, :] = tot[:, :HEAD_W] / jnp.maximum(jnp.abs(tot[:, HEAD_W:]), jnp.exp(-m_t))


def _mlstm_kernel(fb_ref, q_ref, k_ref, v_ref, og_ref, gr_ref, nw_ref, y_ref,
                  hf_ref, hb_ref, cf_ref, cb_ref, mf_ref, mb_ref, *, n_ctx, n_tok):
    L = MLSTM_CHUNK
    nc, ncc = n_tok // L, n_ctx // L
    U = _unroll(nc, MLSTM_UNROLL)
    h = pl.program_id(1)
    fb_f = fb_ref[0, h]
    fb_b = fb_ref[1, h]
    for ref in (cf_ref, cb_ref, mf_ref, mb_ref):
        ref[...] = jnp.zeros_like(ref)

    def body(it, carry):
        sf = [cf_ref[...], mf_ref[...]]
        sb = [cb_ref[...], mb_ref[...]]
        gens = []
        for u in range(U):
            step = it * U + u
            gens.append(_mlstm_chunk(q_ref, k_ref, v_ref, gr_ref, fb_f, sf, hf_ref,
                                     pl.multiple_of(step * L, L), False))
            ck = jnp.where(step < ncc, ncc - 1 - step, nc - 1 - (step - ncc))
            gens.append(_mlstm_chunk(q_ref, k_ref, v_ref, gr_ref, fb_b, sb, hb_ref,
                                     pl.multiple_of(ck * L, L), True))
        _round_robin(gens)
        cf_ref[...], mf_ref[...] = sf
        cb_ref[...], mb_ref[...] = sb
        return carry

    lax.fori_loop(0, nc // U, body, 0)

    fr = 256
    def fin(i, carry):
        rows = pl.ds(pl.multiple_of(i * fr, fr), fr)
        o = jax.nn.sigmoid(og_ref[rows, :]) * (hf_ref[rows, :] + hb_ref[rows, :])
        y_ref[rows, :] = (_rms(o) * nw_ref[...]).astype(y_ref.dtype)
        return carry

    lax.fori_loop(0, n_tok // fr, fin, 0)


def _mlstm(p, grow, f_bias, norm_w, l, n_ctx, cast_src=None):
    b, t, _ = p.shape
    base = 11 * HEADS

    def col(seg):
        return pl.BlockSpec((None, t, HEAD_W), lambda bi, h: (bi, 0, base + seg * HEADS + h))

    return _mixer_call(
        functools.partial(_mlstm_kernel, n_ctx=n_ctx, n_tok=t),
        (b, HEADS),
        [pl.BlockSpec(memory_space=pltpu.SMEM),
         col(0), col(1), col(2), col(3),
         pl.BlockSpec((None, None, 8, t), lambda bi, h: (bi, h, 0, 0)),
         pl.BlockSpec((None, 1, HEAD_W), lambda bi, h: (l, 0, h))],
        pl.BlockSpec((None, t, HEAD_W), lambda bi, h: (bi, 0, h)),
        jax.ShapeDtypeStruct((b, t, GROUP_W), BF16),
        [pltpu.VMEM((t, HEAD_W), F32), pltpu.VMEM((t, HEAD_W), F32),
         pltpu.VMEM((HEAD_W, 2 * HEAD_W), F32), pltpu.VMEM((HEAD_W, 2 * HEAD_W), F32),
         pltpu.VMEM((1, 1), F32), pltpu.VMEM((1, 1), F32)],
        (f_bias, p, p, p, p, grow, norm_w), cast_src, l + 1)


def _outproj_kernel(ya_ref, yb_ref, yc_ref, yd_ref, w_ref, x_ref, mx_ref, mc_ref, o_ref, *, n_ctx, tm):
    i = pl.program_id(1)
    acc = _dot(ya_ref[...], w_ref[0])
    acc += _dot(yb_ref[...], w_ref[1])
    acc += _dot(yc_ref[...], w_ref[2])
    acc += _dot(yd_ref[...], w_ref[3])
    rows = i * tm + lax.broadcasted_iota(jnp.int32, (tm, 1), 0)
    gate = jnp.where(rows < n_ctx, mc_ref[2:3, :], mx_ref[2:3, :])
    o_ref[...] = x_ref[...] + gate * acc


def _outproj(ys, w_out, xs, mods, l, n_ctx):
    b, t, d = xs.shape
    tm, tn = ROW_TILE, 1024
    nb = mods.shape[1] - 1
    yspec = pl.BlockSpec((None, tm, GROUP_W), lambda bi, i, j: (bi, i, 0))
    return pl.pallas_call(
        functools.partial(_outproj_kernel, n_ctx=n_ctx, tm=tm),
        grid=(b, t // tm, d // tn),
        in_specs=[yspec, yspec, yspec, yspec,
                  pl.BlockSpec((4, GROUP_W, tn), lambda bi, i, j: (0, 0, j)),
                  pl.BlockSpec((None, tm, tn), lambda bi, i, j: (bi, i, j)),
                  pl.BlockSpec((None, None, 6, tn), lambda bi, i, j: (l, bi, 0, j)),
                  pl.BlockSpec((None, None, 6, tn), lambda bi, i, j: (l, nb, 0, j))],
        out_specs=pl.BlockSpec((None, tm, tn), lambda bi, i, j: (bi, i, j)),
        out_shape=jax.ShapeDtypeStruct((b, t, d), F32),
        compiler_params=_cparams("arbitrary", "arbitrary", "arbitrary"),
    )(*ys, w_out, xs, mods, mods)


def _ffn_kernel(xm_ref, xp_ref, xn_ref, mx_ref, mc_ref, wa_ref, wg_ref, cwa_ref, cwg_ref,
                cba_ref, cbg_ref, wd_ref, o_ref, hs_ref, ua_ref, ug_ref, *, n_ctx, n_tok, tm):
    assert n_ctx % FFN_SUB == 0 and n_tok % FFN_SUB == 0 and tm % FFN_SUB == 0
    i = pl.program_id(1)
    f = pl.program_id(2)
    nf = pl.num_programs(2) - 1
    row0 = i * tm
    cur = f % 2
    prev = 1 - cur

    def up_project():
        ua_ref[cur] = _dot(hs_ref[...], wa_ref[...])
        ug_ref[cur] = _dot(hs_ref[...], wg_ref[...])

    def finish():
        ua = ua_ref.at[prev]
        ug = ug_ref.at[prev]
        for r in range(0, tm, FFN_SUB):
            g0 = row0 + r
            no_prev = (g0 == 0) | (g0 == n_ctx)
            no_next = (g0 + FFN_SUB == n_ctx) | (g0 + FFN_SUB == n_tok)
            tile_row = lax.broadcasted_iota(jnp.int32, (8, 1), 0)
            drop_first = (tile_row == 0) & no_prev
            drop_last = (tile_row == 7) & no_next

            def conv(u_ref, cw_ref, cb_ref):
                lo = HALO + r
                before = u_ref[lo - 1:lo - 1 + FFN_SUB, :]
                after = u_ref[lo + 1:lo + 1 + FFN_SUB, :]
                before = jnp.concatenate([jnp.where(drop_first, 0.0, before[:8]), before[8:]], axis=0)
                after = jnp.concatenate([after[:-8], jnp.where(drop_last, 0.0, after[-8:])], axis=0)
                return (before * cw_ref[0:1, :] + u_ref[lo:lo + FFN_SUB, :] * cw_ref[1:2, :]
                        + after * cw_ref[2:3, :] + cb_ref[...])

            act = (_silu(conv(ua, cwa_ref, cba_ref)) * conv(ug, cwg_ref, cbg_ref)).astype(BF16)
            o_ref[r:r + FFN_SUB, :] += _dot(act, wd_ref[...])

    @pl.when(f == 0)
    def _():
        hs_ref[0:HALO, :] = _modulate(xp_ref[...], mx_ref, mc_ref, row0 - HALO, 3, n_ctx).astype(BF16)
        _modulate_into(hs_ref, HALO, xm_ref, mx_ref, mc_ref, row0, 3, n_ctx)
        hs_ref[HALO + tm:, :] = _modulate(xn_ref[...], mx_ref, mc_ref, row0 + tm, 3, n_ctx).astype(BF16)
        o_ref[...] = jnp.zeros_like(o_ref)
        up_project()

    @pl.when((f > 0) & (f < nf))
    def _():
        up_project()
        finish()

    @pl.when(f == nf)
    def _():
        finish()
        rows = row0 + lax.broadcasted_iota(jnp.int32, (tm, 1), 0)
        gate = jnp.where(rows < n_ctx, mc_ref[5:6, :], mx_ref[5:6, :])
        o_ref[...] = xm_ref[...] + gate * o_ref[...]


def _ffn(xs, mods, w_up, conv_w, conv_b, w_down, l, n_ctx):
    b, t, d = xs.shape
    tm, tf = FFN_ROW_TILE, FF_TILE
    d_ff = w_down.shape[0]
    nf = d_ff // tf
    nb = mods.shape[1] - 1
    hb = tm // HALO
    last = t // HALO - 1

    def up(f):
        return jnp.minimum(f, nf - 1)

    def fin(f):
        return jnp.maximum(f - 1, 0)

    return pl.pallas_call(
        functools.partial(_ffn_kernel, n_ctx=n_ctx, n_tok=t, tm=tm),
        grid=(b, t // tm, nf + 1),
        in_specs=[
            pl.BlockSpec((None, tm, d), lambda bi, i, f: (bi, i, 0)),
            pl.BlockSpec((None, HALO, d), lambda bi, i, f: (bi, jnp.maximum(i * hb - 1, 0), 0)),
            pl.BlockSpec((None, HALO, d), lambda bi, i, f: (bi, jnp.minimum((i + 1) * hb, last), 0)),
            pl.BlockSpec((None, None, 6, d), lambda bi, i, f: (l, bi, 0, 0)),
            pl.BlockSpec((None, None, 6, d), lambda bi, i, f: (l, nb, 0, 0)),
            pl.BlockSpec((d, tf), lambda bi, i, f: (0, up(f))),
            pl.BlockSpec((d, tf), lambda bi, i, f: (0, nf + up(f))),
            pl.BlockSpec((None, CONV_W, tf), lambda bi, i, f: (l, 0, fin(f))),
            pl.BlockSpec((None, CONV_W, tf), lambda bi, i, f: (l, 0, nf + fin(f))),
            pl.BlockSpec((None, 1, tf), lambda bi, i, f: (l, 0, fin(f))),
            pl.BlockSpec((None, 1, tf), lambda bi, i, f: (l, 0, nf + fin(f))),
            pl.BlockSpec((tf, d), lambda bi, i, f: (fin(f), 0)),
        ],
        out_specs=pl.BlockSpec((None, tm, d), lambda bi, i, f: (bi, i, 0)),
        out_shape=jax.ShapeDtypeStruct((b, t, d), F32),
        scratch_shapes=[pltpu.VMEM((tm + 2 * HALO, d), BF16),
                        pltpu.VMEM((2, tm + 2 * HALO, tf), F32),
                        pltpu.VMEM((2, tm + 2 * HALO, tf), F32)],
        compiler_params=_cparams("arbitrary", "arbitrary", "arbitrary"),
    )(xs, xs, xs, mods, mods, w_up, w_up, conv_w, conv_w, conv_b, conv_b, w_down)


def _final_kernel(x_ref, w_ref, o_ref):
    o_ref[...] = _rms(x_ref[...]) * w_ref[...]


def _final_norm(xs, w, n_ctx):
    b, t, d = xs.shape
    tr = n_ctx
    off = n_ctx // tr
    return pl.pallas_call(
        _final_kernel,
        grid=(b, (t - n_ctx) // tr),
        in_specs=[pl.BlockSpec((None, tr, d), lambda bi, i: (bi, i + off, 0)),
                  pl.BlockSpec((1, d), lambda bi, i: (0, 0))],
        out_specs=pl.BlockSpec((None, tr, d), lambda bi, i: (bi, i, 0)),
        out_shape=jax.ShapeDtypeStruct((b, t - n_ctx, d), F32),
        compiler_params=_cparams("arbitrary", "arbitrary"),
    )(xs, w.reshape(1, d))


def _lambda_init(layer_idx):
    return 0.8 - 0.6 * math.exp(-0.3 * layer_idx)


def _rope_tables(n_ctx, n_lat):
    dh = HEAD_W // 2
    t = np.arange(n_lat)
    row = (t // GRID_W).astype(np.float32)
    colp = (t % GRID_W).astype(np.float32)
    half = dh // 2
    inv = (ROPE_THETA ** (-np.arange(0, half, 2, dtype=np.float32) / half)).astype(np.float32)
    ar = row[:, None] * inv
    ac = colp[:, None] * inv
    ang = jnp.asarray(np.concatenate([ar, ar, ac, ac], axis=-1))
    cos, sin = jnp.cos(ang), jnp.sin(ang)
    sign = np.where((np.arange(dh) % 32) < 16, -1.0, 1.0).astype(np.float32)
    cos = jnp.concatenate([jnp.ones((n_ctx, dh), F32), cos], axis=0)
    sin = jnp.concatenate([jnp.zeros((n_ctx, dh), F32), sin * sign], axis=0)
    return jnp.tile(cos, (1, 2)), jnp.tile(sin, (1, 2))


def _na_bias(na_rpb, kh):
    w = np.arange(GRID_W)
    col_start = np.clip(w - NA_KW // 2, 0, GRID_W - NA_KW)
    col_ok = (w[None, :] >= col_start[:, None]) & (w[None, :] < col_start[:, None] + NA_KW)
    coff = np.clip(w[None, :] - w[:, None], -(NA_KW - 1), NA_KW - 1) + (NA_KW - 1)
    onehot = jnp.asarray(coff[:, :, None] == np.arange(2 * NA_KW - 1), F32)
    toe = jnp.einsum('dhrc,qkc->dhrqk', na_rpb.astype(F32), onehot, precision=lax.Precision.HIGHEST)
    toe = jnp.where(col_ok, toe, NEG).transpose(0, 1, 3, 2, 4)
    shape = (na_rpb.shape[0], na_rpb.shape[1], GRID_W, kh * GRID_W)
    per_id = [toe[:, :, :, NA_KH - 1 - i:NA_KH - 1 - i + kh].reshape(shape) for i in range(kh)]
    return jnp.stack(per_id, axis=2)


def kernel(x, c, ctx, c_ctx, w_ada, b_ada, w_in, hgrn_lb, hgrn_norm_w, diff_lam, diff_norm_w, na_rpb,
           mlstm_f_bias, mlstm_norm_w, w_out, w_up, conv_w, conv_b, w_down, final_norm_w):
    bsz, n_lat, d = x.shape
    n_ctx = ctx.shape[1]
    depth = w_in.shape[0]
    d_ff = w_down.shape[1]
    grid_rows = n_lat // GRID_W

    xs = jnp.concatenate([ctx, x], axis=1)
    n_mod = -(-(bsz + 1) // 8) * 8
    cc = jnp.zeros((n_mod, d), F32).at[:bsz].set(c).at[n_mod - 1].set(c_ctx)
    mods = _ada(cc, w_ada, b_ada).reshape(depth, n_mod, 6, d)

    w_in_all = w_in.astype(BF16)
    w_out_b, w_up_b, w_down_b = (w[0].astype(BF16) for w in (w_out, w_up, w_down))
    w_gate_b = jnp.pad(w_in[:, :, D_MAIN:], ((0, 0), (0, 0), (0, HEAD_W - N_GATES))).astype(BF16)
    conv_b3 = conv_b.reshape(depth, 1, 2 * d_ff)

    lb = jnp.cumsum(jax.nn.softmax(hgrn_lb.astype(F32), axis=0), axis=0)
    lb = lb - lb[:1]
    zero = jnp.zeros_like(lb[:, :1])
    gate_par = jnp.concatenate([jnp.log(lb[:, 0:1]), jnp.log1p(-lb[:, 0:1]), 1.0 - lb[:, 0:1],
                                jnp.log(lb[:, 1:2]), jnp.log1p(-lb[:, 1:2]), 1.0 - lb[:, 1:2],
                                zero, zero], axis=1)
    cos, sin_signed = _rope_tables(n_ctx, n_lat)
    lam = jnp.pad(diff_lam.astype(F32), ((0, 0), (0, 4), (0, HEAD_W - diff_lam.shape[2])))
    bias = _na_bias(na_rpb, min(NA_KH, grid_rows))
    f_bias = mlstm_f_bias.astype(F32)

    def nw(w):
        return w.astype(F32).reshape(depth, 1, GROUP_W)

    hgrn_nw, diff_nw, mlstm_nw = nw(hgrn_norm_w), nw(diff_norm_w), nw(mlstm_norm_w)

    for l in range(depth):
        p, gates = _inproj(xs, mods, w_in_all, w_gate_b, l, n_ctx)
        g4 = gates[:, :, :N_GATES].reshape(bsz, -1, 4, HEADS)
        grow = jnp.pad(g4.transpose(0, 3, 2, 1), ((0, 0), (0, 0), (0, 4), (0, 0)))
        nxt = l + 1 < depth
        ya, w_up_n = _hgrn(p, gate_par, hgrn_nw, l, n_ctx, w_up if nxt else None)
        yb, _ = _diff(p, cos, sin_signed, lam, diff_nw, l, n_ctx, _lambda_init(l))
        yc, w_down_n = _na(p, bias, l, n_ctx, w_down if nxt else None)
        yd, w_out_n = _mlstm(p, grow, f_bias[l], mlstm_nw, l, n_ctx, w_out if nxt else None)
        xs = _outproj((ya, yb, yc, yd), w_out_b.reshape(4, GROUP_W, d), xs, mods, l, n_ctx)
        xs = _ffn(xs, mods, w_up_b, conv_w, conv_b3, w_down_b, l, n_ctx)
        w_out_b, w_up_b, w_down_b = w_out_n, w_up_n, w_down_n
    return _final_norm(xs, final_norm_w, n_ctx)
```

```python
import functools
import math

import numpy as np
import jax
import jax.numpy as jnp
from jax import lax
from jax.experimental import pallas as pl
from jax.experimental.pallas import tpu as pltpu

F32 = jnp.float32
BF16 = jnp.bfloat16

HEADS = 4
HEAD_W = 128
GROUP_W = HEADS * HEAD_W
GRID_W = 64
NA_KH = 8
NA_KW = 16
ROPE_THETA = 10000.0
EPS = 1e-6
CONV_W = 3
D_MAIN = 15 * GROUP_W
N_GATES = 4 * HEADS

VMEM_LIMIT_BYTES = 56 * 1024 * 1024
NEG = -1e30
LOG2E = 1.4426950408889634

HGRN_CHUNK = 64
HGRN_SUB = 16
HGRN_UNROLL = 6
MLSTM_CHUNK = 128
MLSTM_UNROLL = 6
DIFF_Q_TILE = 768
NA_UNROLL = 32
ROW_TILE = 768
INPROJ_TILE = 1280
FF_TILE = 512
HALO = 16
FFN_ROW_TILE = 768
FFN_SUB = 256
MOD_SUB = 256


def _unroll(n, target):
    return max(u for u in range(1, target + 1) if n % u == 0)


def _cparams(*sem):
    return pltpu.CompilerParams(dimension_semantics=sem, vmem_limit_bytes=VMEM_LIMIT_BYTES)


def _dot(a, b):
    return jnp.dot(a, b, preferred_element_type=F32)


def _dot_nt(a, b):
    return lax.dot_general(a, b, (((1,), (1,)), ((), ())), preferred_element_type=F32)


def _dot_tn(a, b):
    return lax.dot_general(a, b, (((0,), (0,)), ((), ())), preferred_element_type=F32)


def _silu(t):
    return t * jax.nn.sigmoid(t)


def _log_sigmoid(z):
    return jnp.minimum(z, 0.0) - jnp.log1p(jnp.exp(-jnp.abs(z)))


def _rms(t):
    return t * lax.rsqrt(jnp.mean(t * t, axis=-1, keepdims=True) + EPS)


def _split3(t):
    hi = t.astype(BF16)
    r1 = t - hi.astype(F32)
    mid = r1.astype(BF16)
    lo = (r1 - mid.astype(F32)).astype(BF16)
    return hi, mid, lo


def _modulate(x, mx_ref, mc_ref, row0, k, n_ctx):
    rows = row0 + lax.broadcasted_iota(jnp.int32, (x.shape[0], 1), 0)
    is_ctx = rows < n_ctx
    shift = jnp.where(is_ctx, mc_ref[k:k + 1, :], mx_ref[k:k + 1, :])
    scale = jnp.where(is_ctx, mc_ref[k + 1:k + 2, :], mx_ref[k + 1:k + 2, :])
    return _rms(x) * (1.0 + scale) + shift


def _modulate_into(dst_ref, dst_off, x_ref, mx_ref, mc_ref, row0, k, n_ctx):
    n = x_ref.shape[0]
    sub = math.gcd(n, MOD_SUB)
    for r in range(0, n, sub):
        h = _modulate(x_ref[r:r + sub, :], mx_ref, mc_ref, row0 + r, k, n_ctx)
        dst_ref[dst_off + r:dst_off + r + sub, :] = h.astype(dst_ref.dtype)


def _ada_kernel(c_ref, w_ref, b_ref, o_ref):
    s = _silu(c_ref[...]).astype(BF16)
    o_ref[...] = _dot(s, w_ref[...].astype(BF16)) + b_ref[...]


def _ada(cc, w_ada, b_ada):
    depth, d, n = w_ada.shape
    tn = 1024
    return pl.pallas_call(
        _ada_kernel,
        grid=(depth, n // tn),
        in_specs=[
            pl.BlockSpec((8, d), lambda l, j: (0, 0)),
            pl.BlockSpec((None, d, tn), lambda l, j: (l, 0, j)),
            pl.BlockSpec((None, 1, tn), lambda l, j: (l, 0, j)),
        ],
        out_specs=pl.BlockSpec((None, 8, tn), lambda l, j: (l, 0, j)),
        out_shape=jax.ShapeDtypeStruct((depth, 8, n), F32),
        compiler_params=_cparams("arbitrary", "arbitrary"),
    )(cc, w_ada, b_ada.reshape(depth, 1, n))


def _inproj_kernel(x_ref, mx_ref, mc_ref, w_ref, wg_ref, p_ref, gate_ref, h_ref, *, n_ctx, tm):
    i = pl.program_id(1)
    j = pl.program_id(2)

    @pl.when(j == 0)
    def _():
        _modulate_into(h_ref, 0, x_ref, mx_ref, mc_ref, i * tm, 0, n_ctx)
        gate_ref[...] = _dot(h_ref[...], wg_ref[...])

    p_ref[...] = _dot(h_ref[...], w_ref[...])


def _inproj(xs, mods, w_in, w_gate, l, n_ctx):
    b, t, d = xs.shape
    tm, tn = ROW_TILE, INPROJ_TILE
    nb = mods.shape[1] - 1
    return pl.pallas_call(
        functools.partial(_inproj_kernel, n_ctx=n_ctx, tm=tm),
        grid=(b, t // tm, D_MAIN // tn),
        in_specs=[
            pl.BlockSpec((None, tm, d), lambda bi, i, j: (bi, i, 0)),
            pl.BlockSpec((None, None, 6, d), lambda bi, i, j: (l, bi, 0, 0)),
            pl.BlockSpec((None, None, 6, d), lambda bi, i, j: (l, nb, 0, 0)),
            pl.BlockSpec((None, d, tn), lambda bi, i, j: (l, 0, j)),
            pl.BlockSpec((None, d, HEAD_W), lambda bi, i, j: (l, 0, 0)),
        ],
        out_specs=[
            pl.BlockSpec((None, tm, tn), lambda bi, i, j: (bi, i, j)),
            pl.BlockSpec((None, tm, HEAD_W), lambda bi, i, j: (bi, i, 0)),
        ],
        out_shape=[
            jax.ShapeDtypeStruct((b, t, D_MAIN), F32),
            jax.ShapeDtypeStruct((b, t, HEAD_W), F32),
        ],
        scratch_shapes=[pltpu.VMEM((tm, d), BF16)],
        compiler_params=_cparams("arbitrary", "arbitrary", "arbitrary"),
    )(xs, mods, mods, w_in, w_gate)


def _mixer_call(kernel_fn, grid, in_specs, out_spec, out_shape, scratch_shapes, operands,
                cast_src=None, cast_layer=0):
    sem = ("arbitrary",) * len(grid)
    if cast_src is None:
        y = pl.pallas_call(kernel_fn, grid=grid, in_specs=in_specs, out_specs=out_spec, out_shape=out_shape,
                           scratch_shapes=scratch_shapes, compiler_params=_cparams(*sem))(*operands)
        return y, None
    n_in = len(in_specs)
    nblk = grid[0] * grid[1]
    rows, cols = cast_src.shape[1:]
    assert rows % (nblk * 16) == 0, (rows, nblk)
    rb = rows // nblk
    src_spec = pl.BlockSpec((None, rb, cols), lambda bi, h, *r: (cast_layer, bi * grid[1] + h, 0))
    dst_spec = pl.BlockSpec((rb, cols), lambda bi, h, *r: (bi * grid[1] + h, 0))

    def kern(*refs):
        src_ref, y_ref, dst_ref = refs[n_in], refs[n_in + 1], refs[n_in + 2]

        def convert():
            dst_ref[...] = src_ref[...].astype(dst_ref.dtype)

        if len(grid) > 2:
            pl.when(pl.program_id(2) == 0)(convert)
        else:
            convert()
        kernel_fn(*refs[:n_in], y_ref, *refs[n_in + 3:])

    return pl.pallas_call(
        kern, grid=grid, in_specs=list(in_specs) + [src_spec], out_specs=[out_spec, dst_spec],
        out_shape=[out_shape, jax.ShapeDtypeStruct((rows, cols), BF16)],
        scratch_shapes=scratch_shapes, compiler_params=_cparams(*sem))(*operands, cast_src)


def _round_robin(gens):
    live = list(gens)
    while live:
        nxt = []
        for g in live:
            try:
                next(g)
                nxt.append(g)
            except StopIteration:
                pass
        live = nxt


def _hgrn_chunk(q_ref, v_ref, z_ref, gp_ref, state, o_ref, r0, rev):
    L, SC = HGRN_CHUNK, HGRN_SUB
    nsub = L // SC
    rows = pl.ds(r0, L)
    q = _silu(q_ref[rows, :]) * (HEAD_W ** -0.5)
    z = z_ref[rows, :]
    v = v_ref[rows, :].astype(BF16)
    e = jnp.exp(-jnp.abs(z))
    inv = 1.0 / (1.0 + e)
    a = gp_ref[1:2, :] + (jnp.minimum(z, 0.0) - jnp.log(1.0 + e))
    ll = gp_ref[0:1, :]
    lf = jnp.maximum(ll, a) + jnp.log(1.0 + jnp.exp(-jnp.abs(ll - a)))
    k = gp_ref[2:3, :] * (jnp.where(z >= 0.0, e, 1.0) * inv)

    t_i = lax.broadcasted_iota(jnp.int32, (L, L), 0)
    s_i = lax.broadcasted_iota(jnp.int32, (L, L), 1)
    t_blk = t_i & ~(SC - 1)
    if not rev:
        local = (s_i <= t_i) & (s_i >= t_blk)
        causal = s_i <= t_i
    else:
        local = (s_i >= t_i) & (s_i < t_blk + SC)
        causal = s_i >= t_i
    sel = jnp.where(local, 1.0, 0.0).astype(BF16)
    hi, mid, lo = _split3(lf)
    bl = _dot(sel, hi) + _dot(sel, mid) + _dot(sel, lo)
    yield

    def blk(x, i):
        return x[i * SC:(i + 1) * SC]

    order = list(range(nsub)) if not rev else list(range(nsub - 1, -1, -1))
    r = [None] * nsub
    acc = jnp.zeros((1, HEAD_W), F32)
    for i in order:
        r[i] = acc
        acc = acc + (bl[(i + 1) * SC - 1:(i + 1) * SC] if not rev else bl[i * SC:i * SC + 1])
    b_end = acc

    qt = q * jnp.exp(bl)
    kinv = k * jnp.exp(-bl)
    kd = jnp.concatenate([blk(kinv, j) * jnp.exp(b_end - r[j]) for j in range(nsub)], axis=0).astype(BF16)
    upd = _dot_tn(v, kd)
    zero = jnp.zeros((SC, HEAD_W), F32)
    parts = []
    for i in range(nsub):
        seen = [j for j in range(nsub) if (j <= i if not rev else j >= i)]
        kk = jnp.concatenate(
            [(blk(kinv, j) if j == i else blk(kinv, j) * jnp.exp(r[i] - r[j])) if j in seen else zero
             for j in range(nsub)], axis=0).astype(BF16)
        parts.append(_dot_nt(blk(qt, i).astype(BF16), kk))
    qb = jnp.concatenate([blk(qt, i) * jnp.exp(r[i]) for i in range(nsub)], axis=0).astype(BF16)
    yield

    st = state[0]
    amat = jnp.where(causal, jnp.concatenate(parts, axis=0), 0.0)
    o_ref[rows, :] = _dot_nt(qb, st.astype(BF16)) + _dot(amat.astype(BF16), v)
    state[0] = st * jnp.exp(b_end) + upd


def _hgrn_kernel(q_ref, v_ref, zf_ref, zb_ref, g_ref, gp_ref, nw_ref, y_ref,
                 of_ref, ob_ref, sf_ref, sb_ref, *, n_ctx, n_tok):
    L = HGRN_CHUNK
    nc, ncc = n_tok // L, n_ctx // L
    U = _unroll(nc, HGRN_UNROLL)
    sf_ref[...] = jnp.zeros_like(sf_ref)
    sb_ref[...] = jnp.zeros_like(sb_ref)

    def body(it, carry):
        sf = [sf_ref[...]]
        sb = [sb_ref[...]]
        gens = []
        for u in range(U):
            step = it * U + u
            gens.append(_hgrn_chunk(q_ref, v_ref, zf_ref, gp_ref.at[0:3], sf, of_ref,
                                    pl.multiple_of(step * L, L), False))
            cb = jnp.where(step < ncc, ncc - 1 - step, nc - 1 - (step - ncc))
            gens.append(_hgrn_chunk(q_ref, v_ref, zb_ref, gp_ref.at[3:6], sb, ob_ref,
                                    pl.multiple_of(cb * L, L), True))
        _round_robin(gens)
        sf_ref[...] = sf[0]
        sb_ref[...] = sb[0]
        return carry

    lax.fori_loop(0, nc // U, body, 0)

    fr = 256
    def fin(i, carry):
        rows = pl.ds(pl.multiple_of(i * fr, fr), fr)
        o = of_ref[rows, :] + ob_ref[rows, :]
        y_ref[rows, :] = (_rms(o) * nw_ref[...] * _silu(g_ref[rows, :])).astype(y_ref.dtype)
        return carry

    lax.fori_loop(0, n_tok // fr, fin, 0)


def _hgrn(p, gate_par, norm_w, l, n_ctx, cast_src=None):
    b, t, _ = p.shape
    nh = GROUP_W // HEAD_W

    def col(seg):
        return pl.BlockSpec((None, t, HEAD_W), lambda bi, h: (bi, 0, seg * nh + h))

    return _mixer_call(
        functools.partial(_hgrn_kernel, n_ctx=n_ctx, n_tok=t),
        (b, HEADS),
        [col(0), col(1), col(2), col(3), col(4),
         pl.BlockSpec((None, 8, HEAD_W), lambda bi, h: (l, 0, h)),
         pl.BlockSpec((None, 1, HEAD_W), lambda bi, h: (l, 0, h))],
        pl.BlockSpec((None, t, HEAD_W), lambda bi, h: (bi, 0, h)),
        jax.ShapeDtypeStruct((b, t, GROUP_W), BF16),
        [pltpu.VMEM((t, HEAD_W), F32), pltpu.VMEM((t, HEAD_W), F32),
         pltpu.VMEM((HEAD_W, HEAD_W), F32), pltpu.VMEM((HEAD_W, HEAD_W), F32)],
        (p, p, p, p, p, gate_par, norm_w), cast_src, l + 1)


def _rope(t, cos, sin_signed):
    lane = lax.broadcasted_iota(jnp.int32, t.shape, 1)
    first = (lane & 31) < 16
    rot = jnp.where(first, pltpu.roll(t, HEAD_W - 16, 1), pltpu.roll(t, 16, 1))
    return t * cos + rot * sin_signed


def _softmax_rows(s):
    e = jnp.exp(s - jnp.max(s, axis=-1, keepdims=True))
    return e * (1.0 / jnp.sum(e, axis=-1, keepdims=True))


def _diff_kernel(q_ref, k_ref, v_ref, cos_ref, sin_ref, lam_ref, nw_ref, y_ref, kb_ref, vb_ref,
                 *, n_ctx, tq, lam_init):
    qi = pl.program_id(2)

    @pl.when(qi == 0)
    def _():
        kb_ref[...] = _rope(k_ref[...], cos_ref[...], sin_ref[...]).astype(BF16)
        v = v_ref[...]
        vb_ref[...] = jnp.concatenate([v, jnp.ones_like(v)], axis=1).astype(BF16)

    lam = lam_ref[...]
    lam_full = (jnp.exp(jnp.sum(lam[0:1] * lam[1:2], axis=-1, keepdims=True))
                - jnp.exp(jnp.sum(lam[2:3] * lam[3:4], axis=-1, keepdims=True)) + lam_init)
    rows = pl.ds(pl.multiple_of(qi * tq, tq), tq)
    q = _rope(q_ref[...], cos_ref[rows, :], sin_ref[rows, :]) * ((HEAD_W // 2) ** -0.5 * LOG2E)
    lane = lax.broadcasted_iota(jnp.int32, q.shape, 1)
    q0 = jnp.where(lane < HEAD_W // 2, q, 0.0).astype(BF16)
    q1 = jnp.where(lane >= HEAD_W // 2, q, 0.0).astype(BF16)

    def attend(r0, r1, kb, vb):
        s0 = _dot_nt(q0[r0:r1], kb)
        s1 = _dot_nt(q1[r0:r1], kb)
        e0 = jnp.exp2(s0 - jnp.max(s0, axis=-1, keepdims=True)).astype(BF16)
        e1 = jnp.exp2(s1 - jnp.max(s1, axis=-1, keepdims=True)).astype(BF16)
        t0 = _dot(e0, vb)
        t1 = _dot(e1, vb)
        o = t0[:, :HEAD_W] / t0[:, HEAD_W:] - lam_full * (t1[:, :HEAD_W] / t1[:, HEAD_W:])
        y_ref[r0:r1, :] = (_rms(o) * nw_ref[...] * (1.0 - lam_init)).astype(y_ref.dtype)

    @pl.when(qi == 0)
    def _():
        attend(0, n_ctx, kb_ref[0:n_ctx, :], vb_ref[0:n_ctx, :])
        if tq > n_ctx:
            attend(n_ctx, tq, kb_ref[...], vb_ref[...])

    @pl.when(qi > 0)
    def _():
        attend(0, tq, kb_ref[...], vb_ref[...])


def _diff(p, cos, sin_signed, lam, norm_w, l, n_ctx, lam_init, cast_src=None):
    b, t, _ = p.shape
    tq = DIFF_Q_TILE if (t % DIFF_Q_TILE == 0 and DIFF_Q_TILE % n_ctx == 0) else n_ctx
    base = 5 * HEADS

    def col(seg, rows, rmap):
        return pl.BlockSpec((None, rows, HEAD_W), lambda bi, h, qi: (bi, rmap(qi), base + seg * HEADS + h))

    return _mixer_call(
        functools.partial(_diff_kernel, n_ctx=n_ctx, tq=tq, lam_init=lam_init),
        (b, HEADS, t // tq),
        [col(0, tq, lambda qi: qi), col(1, t, lambda qi: 0), col(2, t, lambda qi: 0),
         pl.BlockSpec((t, HEAD_W), lambda bi, h, qi: (0, 0)),
         pl.BlockSpec((t, HEAD_W), lambda bi, h, qi: (0, 0)),
         pl.BlockSpec((None, 8, HEAD_W), lambda bi, h, qi: (l, 0, 0)),
         pl.BlockSpec((None, 1, HEAD_W), lambda bi, h, qi: (l, 0, h))],
        pl.BlockSpec((None, tq, HEAD_W), lambda bi, h, qi: (bi, qi, h)),
        jax.ShapeDtypeStruct((b, t, GROUP_W), BF16),
        [pltpu.VMEM((t, HEAD_W), BF16), pltpu.VMEM((t, 2 * HEAD_W), BF16)],
        (p, p, p, cos, sin_signed, lam, norm_w), cast_src, l + 1)


def _na_kernel(q_ref, k_ref, v_ref, bias_ref, y_ref, kb_ref, vb_ref, *, n_ctx, grid_rows, kh):
    scale = HEAD_W ** -0.5
    kb_ref[...] = k_ref[...].astype(BF16)
    vb_ref[...] = v_ref[...].astype(BF16)
    kc = kb_ref[0:n_ctx, :]
    vc = vb_ref[0:n_ctx, :]

    qc = (q_ref[0:n_ctx, :] * scale).astype(BF16)
    pc = _softmax_rows(_dot_nt(qc, kc))
    y_ref[0:n_ctx, :] = _dot(pc.astype(BF16), vc).astype(y_ref.dtype)

    win = kh * GRID_W

    def one_row(r, out):
        rows = pl.ds(pl.multiple_of(n_ctx + r * GRID_W, GRID_W), GRID_W)
        q = (q_ref[rows, :] * scale).astype(BF16)
        start = jnp.clip(r - kh // 2, 0, grid_rows - kh)
        wrows = pl.ds(pl.multiple_of(n_ctx + start * GRID_W, GRID_W), win)
        sb = _dot_nt(q, kb_ref[wrows, :]) + bias_ref[r - start]
        sc = _dot_nt(q, kc)
        yield
        m = jnp.maximum(jnp.max(sb, axis=-1, keepdims=True), jnp.max(sc, axis=-1, keepdims=True))
        eb = jnp.exp(sb - m)
        ec = jnp.exp(sc - m)
        inv = 1.0 / (jnp.sum(eb, axis=-1, keepdims=True) + jnp.sum(ec, axis=-1, keepdims=True))
        pb = (eb * inv).astype(BF16)
        pc_ = (ec * inv).astype(BF16)
        yield
        out.append(_dot(pb, vb_ref[wrows, :]) + _dot(pc_, vc))

    nu = _unroll(grid_rows, NA_UNROLL)

    def body(it, carry):
        outs = [[] for _ in range(nu)]
        _round_robin([one_row(it * nu + u, outs[u]) for u in range(nu)])
        o = jnp.concatenate([x[0] for x in outs], axis=0)
        rows = pl.ds(pl.multiple_of(n_ctx + it * (nu * GRID_W), GRID_W), nu * GRID_W)
        y_ref[rows, :] = o.astype(y_ref.dtype)
        return carry

    lax.fori_loop(0, grid_rows // nu, body, 0)


def _na(p, bias, l, n_ctx, cast_src=None):
    b, t, _ = p.shape
    grid_rows = (t - n_ctx) // GRID_W
    kh = min(NA_KH, grid_rows)
    base = 8 * HEADS

    def col(seg):
        return pl.BlockSpec((None, t, HEAD_W), lambda bi, h: (bi, 0, base + seg * HEADS + h))

    return _mixer_call(
        functools.partial(_na_kernel, n_ctx=n_ctx, grid_rows=grid_rows, kh=kh),
        (b, HEADS),
        [col(0), col(1), col(2),
         pl.BlockSpec((None, None, kh, GRID_W, kh * GRID_W), lambda bi, h: (l, h, 0, 0, 0))],
        pl.BlockSpec((None, t, HEAD_W), lambda bi, h: (bi, 0, h)),
        jax.ShapeDtypeStruct((b, t, GROUP_W), BF16),
        [pltpu.VMEM((t, HEAD_W), BF16), pltpu.VMEM((t, HEAD_W), BF16)],
        (p, p, p, bias), cast_src, l + 1)


def _mlstm_chunk(q_ref, k_ref, v_ref, gr_ref, fb, state, h_ref, r0, rev):
    L = MLSTM_CHUNK
    rows = pl.ds(r0, L)
    gi, gf = (2, 3) if rev else (0, 1)
    q = (q_ref[rows, :] * (HEAD_W ** -0.5)).astype(BF16)
    k = k_ref[rows, :]
    v = v_ref[rows, :]
    vaug = jnp.concatenate([v, jnp.ones_like(v)], axis=1).astype(BF16)
    grow = gr_ref[:, rows]
    i_row = grow[gi:gi + 1, :]
    f_rows = _log_sigmoid(grow + fb)

    t_i = lax.broadcasted_iota(jnp.int32, (L, L), 0)
    s_i = lax.broadcasted_iota(jnp.int32, (L, L), 1)
    causal = (s_i <= t_i) if not rev else (s_i >= t_i)
    b_col = jnp.sum(jnp.where(causal, f_rows[gf:gf + 1, :], 0.0), axis=1, keepdims=True)
    tri_t = jnp.where((t_i <= s_i) if not rev else (t_i >= s_i), 1.0, 0.0).astype(BF16)
    fr3 = _split3(f_rows)
    b_row = (_dot(fr3[0], tri_t) + _dot(fr3[1], tri_t) + _dot(fr3[2], tri_t))[gf:gf + 1, :]
    b_tot = b_row[:, L - 1:L] if not rev else b_row[:, 0:1]
    log_w = jnp.where(causal, b_col - b_row + i_row, NEG)
    row_max = jnp.max(log_w, axis=1, keepdims=True)
    upd_row = b_tot - b_row + i_row
    upd_max = jnp.max(upd_row, axis=1, keepdims=True)
    s_qk = _dot_nt(q, k.astype(BF16))
    k_t = k.T
    yield

    m = state[1]
    inter = b_col + m
    m_t = jnp.maximum(inter, row_max)
    w_inter = jnp.exp(inter - m_t)
    w_intra = (jnp.exp(log_w - m_t) * s_qk).astype(BF16)
    m_new = jnp.maximum(b_tot + m, upd_max)
    w_s = jnp.exp(b_tot + m - m_new)
    state[1] = m_new
    upd = _dot((k_t * jnp.exp(upd_row - m_new)).astype(BF16), vaug)
    intra = _dot(w_intra, vaug)
    yield

    c_aug = state[0]
    tot = w_inter * _dot(q, c_aug.astype(BF16)) + intra
    state[0] = w_s * c_aug + upd
    h_ref[rows, :] = tot[:, :HEAD_W] / jnp.maximum(jnp.abs(tot[:, HEAD_W:]), jnp.exp(-m_t))


def _mlstm_kernel(fb_ref, q_ref, k_ref, v_ref, og_ref, gr_ref, nw_ref, y_ref,
                  hf_ref, hb_ref, cf_ref, cb_ref, mf_ref, mb_ref, *, n_ctx, n_tok):
    L = MLSTM_CHUNK
    nc, ncc = n_tok // L, n_ctx // L
    U = _unroll(nc, MLSTM_UNROLL)
    h = pl.program_id(1)
    fb_f = fb_ref[0, h]
    fb_b = fb_ref[1, h]
    for ref in (cf_ref, cb_ref, mf_ref, mb_ref):
        ref[...] = jnp.zeros_like(ref)

    def body(it, carry):
        sf = [cf_ref[...], mf_ref[...]]
        sb = [cb_ref[...], mb_ref[...]]
        gens = []
        for u in range(U):
            step = it * U + u
            gens.append(_mlstm_chunk(q_ref, k_ref, v_ref, gr_ref, fb_f, sf, hf_ref,
                                     pl.multiple_of(step * L, L), False))
            ck = jnp.where(step < ncc, ncc - 1 - step, nc - 1 - (step - ncc))
            gens.append(_mlstm_chunk(q_ref, k_ref, v_ref, gr_ref, fb_b, sb, hb_ref,
                                     pl.multiple_of(ck * L, L), True))
        _round_robin(gens)
        cf_ref[...], mf_ref[...] = sf
        cb_ref[...], mb_ref[...] = sb
        return carry

    lax.fori_loop(0, nc // U, body, 0)

    fr = 256
    def fin(i, carry):
        rows = pl.ds(pl.multiple_of(i * fr, fr), fr)
        o = jax.nn.sigmoid(og_ref[rows, :]) * (hf_ref[rows, :] + hb_ref[rows, :])
        y_ref[rows, :] = (_rms(o) * nw_ref[...]).astype(y_ref.dtype)
        return carry

    lax.fori_loop(0, n_tok // fr, fin, 0)


def _mlstm(p, grow, f_bias, norm_w, l, n_ctx, cast_src=None):
    b, t, _ = p.shape
    base = 11 * HEADS

    def col(seg):
        return pl.BlockSpec((None, t, HEAD_W), lambda bi, h: (bi, 0, base + seg * HEADS + h))

    return _mixer_call(
        functools.partial(_mlstm_kernel, n_ctx=n_ctx, n_tok=t),
        (b, HEADS),
        [pl.BlockSpec(memory_space=pltpu.SMEM),
         col(0), col(1), col(2), col(3),
         pl.BlockSpec((None, None, 8, t), lambda bi, h: (bi, h, 0, 0)),
         pl.BlockSpec((None, 1, HEAD_W), lambda bi, h: (l, 0, h))],
        pl.BlockSpec((None, t, HEAD_W), lambda bi, h: (bi, 0, h)),
        jax.ShapeDtypeStruct((b, t, GROUP_W), BF16),
        [pltpu.VMEM((t, HEAD_W), F32), pltpu.VMEM((t, HEAD_W), F32),
         pltpu.VMEM((HEAD_W, 2 * HEAD_W), F32), pltpu.VMEM((HEAD_W, 2 * HEAD_W), F32),
         pltpu.VMEM((1, 1), F32), pltpu.VMEM((1, 1), F32)],
        (f_bias, p, p, p, p, grow, norm_w), cast_src, l + 1)


def _outproj_kernel(ya_ref, yb_ref, yc_ref, yd_ref, w_ref, x_ref, mx_ref, mc_ref, o_ref, *, n_ctx, tm):
    i = pl.program_id(1)
    acc = _dot(ya_ref[...], w_ref[0])
    acc += _dot(yb_ref[...], w_ref[1])
    acc += _dot(yc_ref[...], w_ref[2])
    acc += _dot(yd_ref[...], w_ref[3])
    rows = i * tm + lax.broadcasted_iota(jnp.int32, (tm, 1), 0)
    gate = jnp.where(rows < n_ctx, mc_ref[2:3, :], mx_ref[2:3, :])
    o_ref[...] = x_ref[...] + gate * acc


def _outproj(ys, w_out, xs, mods, l, n_ctx):
    b, t, d = xs.shape
    tm, tn = ROW_TILE, 1024
    nb = mods.shape[1] - 1
    yspec = pl.BlockSpec((None, tm, GROUP_W), lambda bi, i, j: (bi, i, 0))
    return pl.pallas_call(
        functools.partial(_outproj_kernel, n_ctx=n_ctx, tm=tm),
        grid=(b, t // tm, d // tn),
        in_specs=[yspec, yspec, yspec, yspec,
                  pl.BlockSpec((4, GROUP_W, tn), lambda bi, i, j: (0, 0, j)),
                  pl.BlockSpec((None, tm, tn), lambda bi, i, j: (bi, i, j)),
                  pl.BlockSpec((None, None, 6, tn), lambda bi, i, j: (l, bi, 0, j)),
                  pl.BlockSpec((None, None, 6, tn), lambda bi, i, j: (l, nb, 0, j))],
        out_specs=pl.BlockSpec((None, tm, tn), lambda bi, i, j: (bi, i, j)),
        out_shape=jax.ShapeDtypeStruct((b, t, d), F32),
        compiler_params=_cparams("arbitrary", "arbitrary", "arbitrary"),
    )(*ys, w_out, xs, mods, mods)


def _ffn_kernel(xm_ref, xp_ref, xn_ref, mx_ref, mc_ref, wa_ref, wg_ref, cwa_ref, cwg_ref,
                cba_ref, cbg_ref, wd_ref, o_ref, hs_ref, ua_ref, ug_ref, *, n_ctx, n_tok, tm):
    assert n_ctx % FFN_SUB == 0 and n_tok % FFN_SUB == 0 and tm % FFN_SUB == 0
    i = pl.program_id(1)
    f = pl.program_id(2)
    nf = pl.num_programs(2) - 1
    row0 = i * tm
    cur = f % 2
    prev = 1 - cur

    def up_project():
        ua_ref[cur] = _dot(hs_ref[...], wa_ref[...])
        ug_ref[cur] = _dot(hs_ref[...], wg_ref[...])

    def finish():
        ua = ua_ref.at[prev]
        ug = ug_ref.at[prev]
        for r in range(0, tm, FFN_SUB):
            g0 = row0 + r
            no_prev = (g0 == 0) | (g0 == n_ctx)
            no_next = (g0 + FFN_SUB == n_ctx) | (g0 + FFN_SUB == n_tok)
            tile_row = lax.broadcasted_iota(jnp.int32, (8, 1), 0)
            drop_first = (tile_row == 0) & no_prev
            drop_last = (tile_row == 7) & no_next

            def conv(u_ref, cw_ref, cb_ref):
                lo = HALO + r
                before = u_ref[lo - 1:lo - 1 + FFN_SUB, :]
                after = u_ref[lo + 1:lo + 1 + FFN_SUB, :]
                before = jnp.concatenate([jnp.where(drop_first, 0.0, before[:8]), before[8:]], axis=0)
                after = jnp.concatenate([after[:-8], jnp.where(drop_last, 0.0, after[-8:])], axis=0)
                return (before * cw_ref[0:1, :] + u_ref[lo:lo + FFN_SUB, :] * cw_ref[1:2, :]
                        + after * cw_ref[2:3, :] + cb_ref[...])

            act = (_silu(conv(ua, cwa_ref, cba_ref)) * conv(ug, cwg_ref, cbg_ref)).astype(BF16)
            o_ref[r:r + FFN_SUB, :] += _dot(act, wd_ref[...])

    @pl.when(f == 0)
    def _():
        hs_ref[0:HALO, :] = _modulate(xp_ref[...], mx_ref, mc_ref, row0 - HALO, 3, n_ctx).astype(BF16)
        _modulate_into(hs_ref, HALO, xm_ref, mx_ref, mc_ref, row0, 3, n_ctx)
        hs_ref[HALO + tm:, :] = _modulate(xn_ref[...], mx_ref, mc_ref, row0 + tm, 3, n_ctx).astype(BF16)
        o_ref[...] = jnp.zeros_like(o_ref)
        up_project()

    @pl.when((f > 0) & (f < nf))
    def _():
        up_project()
        finish()

    @pl.when(f == nf)
    def _():
        finish()
        rows = row0 + lax.broadcasted_iota(jnp.int32, (tm, 1), 0)
        gate = jnp.where(rows < n_ctx, mc_ref[5:6, :], mx_ref[5:6, :])
        o_ref[...] = xm_ref[...] + gate * o_ref[...]


def _ffn(xs, mods, w_up, conv_w, conv_b, w_down, l, n_ctx):
    b, t, d = xs.shape
    tm, tf = FFN_ROW_TILE, FF_TILE
    d_ff = w_down.shape[0]
    nf = d_ff // tf
    nb = mods.shape[1] - 1
    hb = tm // HALO
    last = t // HALO - 1

    def up(f):
        return jnp.minimum(f, nf - 1)

    def fin(f):
        return jnp.maximum(f - 1, 0)

    return pl.pallas_call(
        functools.partial(_ffn_kernel, n_ctx=n_ctx, n_tok=t, tm=tm),
        grid=(b, t // tm, nf + 1),
        in_specs=[
            pl.BlockSpec((None, tm, d), lambda bi, i, f: (bi, i, 0)),
            pl.BlockSpec((None, HALO, d), lambda bi, i, f: (bi, jnp.maximum(i * hb - 1, 0), 0)),
            pl.BlockSpec((None, HALO, d), lambda bi, i, f: (bi, jnp.minimum((i + 1) * hb, last), 0)),
            pl.BlockSpec((None, None, 6, d), lambda bi, i, f: (l, bi, 0, 0)),
            pl.BlockSpec((None, None, 6, d), lambda bi, i, f: (l, nb, 0, 0)),
            pl.BlockSpec((d, tf), lambda bi, i, f: (0, up(f))),
            pl.BlockSpec((d, tf), lambda bi, i, f: (0, nf + up(f))),
            pl.BlockSpec((None, CONV_W, tf), lambda bi, i, f: (l, 0, fin(f))),
            pl.BlockSpec((None, CONV_W, tf), lambda bi, i, f: (l, 0, nf + fin(f))),
            pl.BlockSpec((None, 1, tf), lambda bi, i, f: (l, 0, fin(f))),
            pl.BlockSpec((None, 1, tf), lambda bi, i, f: (l, 0, nf + fin(f))),
            pl.BlockSpec((tf, d), lambda bi, i, f: (fin(f), 0)),
        ],
        out_specs=pl.BlockSpec((None, tm, d), lambda bi, i, f: (bi, i, 0)),
        out_shape=jax.ShapeDtypeStruct((b, t, d), F32),
        scratch_shapes=[pltpu.VMEM((tm + 2 * HALO, d), BF16),
                        pltpu.VMEM((2, tm + 2 * HALO, tf), F32),
                        pltpu.VMEM((2, tm + 2 * HALO, tf), F32)],
        compiler_params=_cparams("arbitrary", "arbitrary", "arbitrary"),
    )(xs, xs, xs, mods, mods, w_up, w_up, conv_w, conv_w, conv_b, conv_b, w_down)


def _final_kernel(x_ref, w_ref, o_ref):
    o_ref[...] = _rms(x_ref[...]) * w_ref[...]


def _final_norm(xs, w, n_ctx):
    b, t, d = xs.shape
    tr = n_ctx
    off = n_ctx // tr
    return pl.pallas_call(
        _final_kernel,
        grid=(b, (t - n_ctx) // tr),
        in_specs=[pl.BlockSpec((None, tr, d), lambda bi, i: (bi, i + off, 0)),
                  pl.BlockSpec((1, d), lambda bi, i: (0, 0))],
        out_specs=pl.BlockSpec((None, tr, d), lambda bi, i: (bi, i, 0)),
        out_shape=jax.ShapeDtypeStruct((b, t - n_ctx, d), F32),
        compiler_params=_cparams("arbitrary", "arbitrary"),
    )(xs, w.reshape(1, d))


def _lambda_init(layer_idx):
    return 0.8 - 0.6 * math.exp(-0.3 * layer_idx)


def _rope_tables(n_ctx, n_lat):
    dh = HEAD_W // 2
    t = np.arange(n_lat)
    row = (t // GRID_W).astype(np.float32)
    colp = (t % GRID_W).astype(np.float32)
    half = dh // 2
    inv = (ROPE_THETA ** (-np.arange(0, half, 2, dtype=np.float32) / half)).astype(np.float32)
    ar = row[:, None] * inv
    ac = colp[:, None] * inv
    ang = jnp.asarray(np.concatenate([ar, ar, ac, ac], axis=-1))
    cos, sin = jnp.cos(ang), jnp.sin(ang)
    sign = np.where((np.arange(dh) % 32) < 16, -1.0, 1.0).astype(np.float32)
    cos = jnp.concatenate([jnp.ones((n_ctx, dh), F32), cos], axis=0)
    sin = jnp.concatenate([jnp.zeros((n_ctx, dh), F32), sin * sign], axis=0)
    return jnp.tile(cos, (1, 2)), jnp.tile(sin, (1, 2))


def _na_bias(na_rpb, kh):
    w = np.arange(GRID_W)
    col_start = np.clip(w - NA_KW // 2, 0, GRID_W - NA_KW)
    col_ok = (w[None, :] >= col_start[:, None]) & (w[None, :] < col_start[:, None] + NA_KW)
    coff = np.clip(w[None, :] - w[:, None], -(NA_KW - 1), NA_KW - 1) + (NA_KW - 1)
    onehot = jnp.asarray(coff[:, :, None] == np.arange(2 * NA_KW - 1), F32)
    toe = jnp.einsum('dhrc,qkc->dhrqk', na_rpb.astype(F32), onehot, precision=lax.Precision.HIGHEST)
    toe = jnp.where(col_ok, toe, NEG).transpose(0, 1, 3, 2, 4)
    shape = (na_rpb.shape[0], na_rpb.shape[1], GRID_W, kh * GRID_W)
    per_id = [toe[:, :, :, NA_KH - 1 - i:NA_KH - 1 - i + kh].reshape(shape) for i in range(kh)]
    return jnp.stack(per_id, axis=2)


def kernel(x, c, ctx, c_ctx, w_ada, b_ada, w_in, hgrn_lb, hgrn_norm_w, diff_lam, diff_norm_w, na_rpb,
           mlstm_f_bias, mlstm_norm_w, w_out, w_up, conv_w, conv_b, w_down, final_norm_w):
    bsz, n_lat, d = x.shape
    n_ctx = ctx.shape[1]
    depth = w_in.shape[0]
    d_ff = w_down.shape[1]
    grid_rows = n_lat // GRID_W

    xs = jnp.concatenate([ctx, x], axis=1)
    n_mod = -(-(bsz + 1) // 8) * 8
    cc = jnp.zeros((n_mod, d), F32).at[:bsz].set(c).at[n_mod - 1].set(c_ctx)
    mods = _ada(cc, w_ada, b_ada).reshape(depth, n_mod, 6, d)

    w_in_all = w_in.astype(BF16)
    w_out_b, w_up_b, w_down_b = (w[0].astype(BF16) for w in (w_out, w_up, w_down))
    w_gate_b = jnp.pad(w_in[:, :, D_MAIN:], ((0, 0), (0, 0), (0, HEAD_W - N_GATES))).astype(BF16)
    conv_b3 = conv_b.reshape(depth, 1, 2 * d_ff)

    lb = jnp.cumsum(jax.nn.softmax(hgrn_lb.astype(F32), axis=0), axis=0)
    lb = lb - lb[:1]
    zero = jnp.zeros_like(lb[:, :1])
    gate_par = jnp.concatenate([jnp.log(lb[:, 0:1]), jnp.log1p(-lb[:, 0:1]), 1.0 - lb[:, 0:1],
                                jnp.log(lb[:, 1:2]), jnp.log1p(-lb[:, 1:2]), 1.0 - lb[:, 1:2],
                                zero, zero], axis=1)
    cos, sin_signed = _rope_tables(n_ctx, n_lat)
    lam = jnp.pad(diff_lam.astype(F32), ((0, 0), (0, 4), (0, HEAD_W - diff_lam.shape[2])))
    bias = _na_bias(na_rpb, min(NA_KH, grid_rows))
    f_bias = mlstm_f_bias.astype(F32)

    def nw(w):
        return w.astype(F32).reshape(depth, 1, GROUP_W)

    hgrn_nw, diff_nw, mlstm_nw = nw(hgrn_norm_w), nw(diff_norm_w), nw(mlstm_norm_w)

    for l in range(depth):
        p, gates = _inproj(xs, mods, w_in_all, w_gate_b, l, n_ctx)
        g4 = gates[:, :, :N_GATES].reshape(bsz, -1, 4, HEADS)
        grow = jnp.pad(g4.transpose(0, 3, 2, 1), ((0, 0), (0, 0), (0, 4), (0, 0)))
        nxt = l + 1 < depth
        ya, w_up_n = _hgrn(p, gate_par, hgrn_nw, l, n_ctx, w_up if nxt else None)
        yb, _ = _diff(p, cos, sin_signed, lam, diff_nw, l, n_ctx, _lambda_init(l))
        yc, w_down_n = _na(p, bias, l, n_ctx, w_down if nxt else None)
        yd, w_out_n = _mlstm(p, grow, f_bias[l], mlstm_nw, l, n_ctx, w_out if nxt else None)
        xs = _outproj((ya, yb, yc, yd), w_out_b.reshape(4, GROUP_W, d), xs, mods, l, n_ctx)
        xs = _ffn(xs, mods, w_up_b, conv_w, conv_b3, w_down_b, l, n_ctx)
        w_out_b, w_up_b, w_down_b = w_out_n, w_up_n, w_down_n
    return _final_norm(xs, final_norm_w, n_ctx)
```

```python
import functools
import math

import numpy as np
import jax
import jax.numpy as jnp
from jax import lax
from jax.experimental import pallas as pl
from jax.experimental.pallas import tpu as pltpu

F32 = jnp.float32
BF16 = jnp.bfloat16

HEADS = 4
HEAD_W = 128
GROUP_W = HEADS * HEAD_W
GRID_W = 64
NA_KH = 8
NA_KW = 16
ROPE_THETA = 10000.0
EPS = 1e-6
CONV_W = 3
D_MAIN = 15 * GROUP_W
N_GATES = 4 * HEADS

VMEM_LIMIT_BYTES = 56 * 1024 * 1024
NEG = -1e30
LOG2E = 1.4426950408889634

HGRN_CHUNK = 64
HGRN_SUB = 16
HGRN_UNROLL = 6
MLSTM_CHUNK = 128
MLSTM_UNROLL = 6
DIFF_Q_TILE = 768
NA_UNROLL = 32
ROW_TILE = 768
INPROJ_TILE = 1280
FF_TILE = 512
HALO = 16
FFN_ROW_TILE = 768
FFN_SUB = 256
MOD_SUB = 256


def _unroll(n, target):
    return max(u for u in range(1, target + 1) if n % u == 0)


def _cparams(*sem):
    return pltpu.CompilerParams(dimension_semantics=sem, vmem_limit_bytes=VMEM_LIMIT_BYTES)


def _dot(a, b):
    return jnp.dot(a, b, preferred_element_type=F32)


def _dot_nt(a, b):
    return lax.dot_general(a, b, (((1,), (1,)), ((), ())), preferred_element_type=F32)


def _dot_tn(a, b):
    return lax.dot_general(a, b, (((0,), (0,)), ((), ())), preferred_element_type=F32)


def _silu(t):
    return t * jax.nn.sigmoid(t)


def _log_sigmoid(z):
    return jnp.minimum(z, 0.0) - jnp.log1p(jnp.exp(-jnp.abs(z)))


def _rms(t):
    return t * lax.rsqrt(jnp.mean(t * t, axis=-1, keepdims=True) + EPS)


def _split3(t):
    hi = t.astype(BF16)
    r1 = t - hi.astype(F32)
    mid = r1.astype(BF16)
    lo = (r1 - mid.astype(F32)).astype(BF16)
    return hi, mid, lo


def _modulate(x, mx_ref, mc_ref, row0, k, n_ctx):
    assert n_ctx % x.shape[0] == 0
    is_ctx = row0 < n_ctx
    shift = jnp.where(is_ctx, mc_ref[k:k + 1, :], mx_ref[k:k + 1, :])
    scale = jnp.where(is_ctx, mc_ref[k + 1:k + 2, :], mx_ref[k + 1:k + 2, :])
    return _rms(x) * (1.0 + scale) + shift


def _modulate_into(dst_ref, dst_off, x_ref, mx_ref, mc_ref, row0, k, n_ctx):
    n = x_ref.shape[0]
    sub = math.gcd(n, MOD_SUB)
    for r in range(0, n, sub):
        h = _modulate(x_ref[r:r + sub, :], mx_ref, mc_ref, row0 + r, k, n_ctx)
        dst_ref[dst_off + r:dst_off + r + sub, :] = h.astype(dst_ref.dtype)


def _ada_kernel(c_ref, w_ref, b_ref, o_ref):
    s = _silu(c_ref[...]).astype(BF16)
    o_ref[...] = _dot(s, w_ref[...].astype(BF16)) + b_ref[...]


def _ada(cc, w_ada, b_ada):
    depth, d, n = w_ada.shape
    tn = 1024
    return pl.pallas_call(
        _ada_kernel,
        grid=(depth, n // tn),
        in_specs=[
            pl.BlockSpec((8, d), lambda l, j: (0, 0)),
            pl.BlockSpec((None, d, tn), lambda l, j: (l, 0, j)),
            pl.BlockSpec((None, 1, tn), lambda l, j: (l, 0, j)),
        ],
        out_specs=pl.BlockSpec((None, 8, tn), lambda l, j: (l, 0, j)),
        out_shape=jax.ShapeDtypeStruct((depth, 8, n), F32),
        compiler_params=_cparams("arbitrary", "arbitrary"),
    )(cc, w_ada, b_ada.reshape(depth, 1, n))


def _inproj_kernel(x_ref, mx_ref, mc_ref, w_ref, wg_ref, p_ref, gate_ref, h_ref, *, n_ctx, tm):
    i = pl.program_id(1)
    j = pl.program_id(2)

    @pl.when(j == 0)
    def _():
        _modulate_into(h_ref, 0, x_ref, mx_ref, mc_ref, i * tm, 0, n_ctx)
        gate_ref[...] = _dot(h_ref[...], wg_ref[...])

    p_ref[...] = _dot(h_ref[...], w_ref[...])


def _inproj(xs, mods, w_in, w_gate, l, n_ctx):
    b, t, d = xs.shape
    tm, tn = ROW_TILE, INPROJ_TILE
    nb = mods.shape[1] - 1
    return pl.pallas_call(
        functools.partial(_inproj_kernel, n_ctx=n_ctx, tm=tm),
        grid=(b, t // tm, D_MAIN // tn),
        in_specs=[
            pl.BlockSpec((None, tm, d), lambda bi, i, j: (bi, i, 0)),
            pl.BlockSpec((None, None, 6, d), lambda bi, i, j: (l, bi, 0, 0)),
            pl.BlockSpec((None, None, 6, d), lambda bi, i, j: (l, nb, 0, 0)),
            pl.BlockSpec((None, d, tn), lambda bi, i, j: (l, 0, j)),
            pl.BlockSpec((None, d, HEAD_W), lambda bi, i, j: (l, 0, 0)),
        ],
        out_specs=[
            pl.BlockSpec((None, tm, tn), lambda bi, i, j: (bi, i, j)),
            pl.BlockSpec((None, tm, HEAD_W), lambda bi, i, j: (bi, i, 0)),
        ],
        out_shape=[
            jax.ShapeDtypeStruct((b, t, D_MAIN), F32),
            jax.ShapeDtypeStruct((b, t, HEAD_W), F32),
        ],
        scratch_shapes=[pltpu.VMEM((tm, d), BF16)],
        compiler_params=_cparams("arbitrary", "arbitrary", "arbitrary"),
    )(xs, mods, mods, w_in, w_gate)


def _mixer_call(kernel_fn, grid, in_specs, out_spec, out_shape, scratch_shapes, operands,
                cast_src=None, cast_layer=0):
    sem = ("arbitrary",) * len(grid)
    if cast_src is None:
        y = pl.pallas_call(kernel_fn, grid=grid, in_specs=in_specs, out_specs=out_spec, out_shape=out_shape,
                           scratch_shapes=scratch_shapes, compiler_params=_cparams(*sem))(*operands)
        return y, None
    n_in = len(in_specs)
    nblk = grid[0] * grid[1]
    rows, cols = cast_src.shape[1:]
    assert rows % (nblk * 16) == 0, (rows, nblk)
    rb = rows // nblk
    src_spec = pl.BlockSpec((None, rb, cols), lambda bi, h, *r: (cast_layer, bi * grid[1] + h, 0))
    dst_spec = pl.BlockSpec((rb, cols), lambda bi, h, *r: (bi * grid[1] + h, 0))

    def kern(*refs):
        src_ref, y_ref, dst_ref = refs[n_in], refs[n_in + 1], refs[n_in + 2]

        def convert():
            dst_ref[...] = src_ref[...].astype(dst_ref.dtype)

        if len(grid) > 2:
            pl.when(pl.program_id(2) == 0)(convert)
        else:
            convert()
        kernel_fn(*refs[:n_in], y_ref, *refs[n_in + 3:])

    return pl.pallas_call(
        kern, grid=grid, in_specs=list(in_specs) + [src_spec], out_specs=[out_spec, dst_spec],
        out_shape=[out_shape, jax.ShapeDtypeStruct((rows, cols), BF16)],
        scratch_shapes=scratch_shapes, compiler_params=_cparams(*sem))(*operands, cast_src)


def _round_robin(gens):
    live = list(gens)
    while live:
        nxt = []
        for g in live:
            try:
                next(g)
                nxt.append(g)
            except StopIteration:
                pass
        live = nxt


def _hgrn_chunk(q_ref, v_ref, z_ref, gp_ref, state, o_ref, r0, rev):
    L, SC = HGRN_CHUNK, HGRN_SUB
    nsub = L // SC
    rows = pl.ds(r0, L)
    q = _silu(q_ref[rows, :]) * (HEAD_W ** -0.5)
    z = z_ref[rows, :]
    v = v_ref[rows, :].astype(BF16)
    e = jnp.exp(-jnp.abs(z))
    inv = 1.0 / (1.0 + e)
    a = gp_ref[1:2, :] + (jnp.minimum(z, 0.0) - jnp.log(1.0 + e))
    ll = gp_ref[0:1, :]
    lf = jnp.maximum(ll, a) + jnp.log(1.0 + jnp.exp(-jnp.abs(ll - a)))
    k = gp_ref[2:3, :] * (jnp.where(z >= 0.0, e, 1.0) * inv)

    t_i = lax.broadcasted_iota(jnp.int32, (L, L), 0)
    s_i = lax.broadcasted_iota(jnp.int32, (L, L), 1)
    t_blk = t_i & ~(SC - 1)
    if not rev:
        local = (s_i <= t_i) & (s_i >= t_blk)
        causal = s_i <= t_i
    else:
        local = (s_i >= t_i) & (s_i < t_blk + SC)
        causal = s_i >= t_i
    sel = jnp.where(local, 1.0, 0.0).astype(BF16)
    hi, mid, lo = _split3(lf)
    bl = _dot(sel, hi) + _dot(sel, mid) + _dot(sel, lo)
    yield

    def blk(x, i):
        return x[i * SC:(i + 1) * SC]

    order = list(range(nsub)) if not rev else list(range(nsub - 1, -1, -1))
    r = [None] * nsub
    acc = jnp.zeros((1, HEAD_W), F32)
    for i in order:
        r[i] = acc
        acc = acc + (bl[(i + 1) * SC - 1:(i + 1) * SC] if not rev else bl[i * SC:i * SC + 1])
    b_end = acc

    qt = q * jnp.exp(bl)
    kinv = k * jnp.exp(-bl)
    kd = jnp.concatenate([blk(kinv, j) * jnp.exp(b_end - r[j]) for j in range(nsub)], axis=0).astype(BF16)
    upd = _dot_tn(v, kd)
    zero = jnp.zeros((SC, HEAD_W), F32)
    parts = []
    for i in range(nsub):
        seen = [j for j in range(nsub) if (j <= i if not rev else j >= i)]
        kk = jnp.concatenate(
            [(blk(kinv, j) if j == i else blk(kinv, j) * jnp.exp(r[i] - r[j])) if j in seen else zero
             for j in range(nsub)], axis=0).astype(BF16)
        parts.append(_dot_nt(blk(qt, i).astype(BF16), kk))
    qb = jnp.concatenate([blk(qt, i) * jnp.exp(r[i]) for i in range(nsub)], axis=0).astype(BF16)
    yield

    st = state[0]
    amat = jnp.where(causal, jnp.concatenate(parts, axis=0), 0.0)
    o_ref[rows, :] = _dot_nt(qb, st.astype(BF16)) + _dot(amat.astype(BF16), v)
    state[0] = st * jnp.exp(b_end) + upd


def _hgrn_kernel(q_ref, v_ref, zf_ref, zb_ref, g_ref, gp_ref, nw_ref, y_ref,
                 of_ref, ob_ref, sf_ref, sb_ref, *, n_ctx, n_tok):
    L = HGRN_CHUNK
    nc, ncc = n_tok // L, n_ctx // L
    U = _unroll(nc, HGRN_UNROLL)
    sf_ref[...] = jnp.zeros_like(sf_ref)
    sb_ref[...] = jnp.zeros_like(sb_ref)

    def body(it, carry):
        sf = [sf_ref[...]]
        sb = [sb_ref[...]]
        gens = []
        for u in range(U):
            step = it * U + u
            gens.append(_hgrn_chunk(q_ref, v_ref, zf_ref, gp_ref.at[0:3], sf, of_ref,
                                    pl.multiple_of(step * L, L), False))
            cb = jnp.where(step < ncc, ncc - 1 - step, nc - 1 - (step - ncc))
            gens.append(_hgrn_chunk(q_ref, v_ref, zb_ref, gp_ref.at[3:6], sb, ob_ref,
                                    pl.multiple_of(cb * L, L), True))
        _round_robin(gens)
        sf_ref[...] = sf[0]
        sb_ref[...] = sb[0]
        return carry

    lax.fori_loop(0, nc // U, body, 0)

    fr = 256
    def fin(i, carry):
        rows = pl.ds(pl.multiple_of(i * fr, fr), fr)
        o = of_ref[rows, :] + ob_ref[rows, :]
        y_ref[rows, :] = (_rms(o) * nw_ref[...] * _silu(g_ref[rows, :])).astype(y_ref.dtype)
        return carry

    lax.fori_loop(0, n_tok // fr, fin, 0)


def _hgrn(p, gate_par, norm_w, l, n_ctx, cast_src=None):
    b, t, _ = p.shape
    nh = GROUP_W // HEAD_W

    def col(seg):
        return pl.BlockSpec((None, t, HEAD_W), lambda bi, h: (bi, 0, seg * nh + h))

    return _mixer_call(
        functools.partial(_hgrn_kernel, n_ctx=n_ctx, n_tok=t),
        (b, HEADS),
        [col(0), col(1), col(2), col(3), col(4),
         pl.BlockSpec((None, 8, HEAD_W), lambda bi, h: (l, 0, h)),
         pl.BlockSpec((None, 1, HEAD_W), lambda bi, h: (l, 0, h))],
        pl.BlockSpec((None, t, HEAD_W), lambda bi, h: (bi, 0, h)),
        jax.ShapeDtypeStruct((b, t, GROUP_W), BF16),
        [pltpu.VMEM((t, HEAD_W), F32), pltpu.VMEM((t, HEAD_W), F32),
         pltpu.VMEM((HEAD_W, HEAD_W), F32), pltpu.VMEM((HEAD_W, HEAD_W), F32)],
        (p, p, p, p, p, gate_par, norm_w), cast_src, l + 1)


def _rope(t, cos, sin_signed):
    lane = lax.broadcasted_iota(jnp.int32, t.shape, 1)
    first = (lane & 31) < 16
    rot = jnp.where(first, pltpu.roll(t, HEAD_W - 16, 1), pltpu.roll(t, 16, 1))
    return t * cos + rot * sin_signed


def _softmax_rows(s):
    e = jnp.exp(s - jnp.max(s, axis=-1, keepdims=True))
    return e * (1.0 / jnp.sum(e, axis=-1, keepdims=True))


def _diff_kernel(q_ref, k_ref, v_ref, cos_ref, sin_ref, lam_ref, nw_ref, y_ref, kb_ref, vb_ref,
                 *, n_ctx, tq, lam_init):
    qi = pl.program_id(2)

    @pl.when(qi == 0)
    def _():
        kb_ref[...] = _rope(k_ref[...], cos_ref[...], sin_ref[...]).astype(BF16)
        v = v_ref[...]
        vb_ref[...] = jnp.concatenate([v, jnp.ones_like(v)], axis=1).astype(BF16)

    lam = lam_ref[...]
    lam_full = (jnp.exp(jnp.sum(lam[0:1] * lam[1:2], axis=-1, keepdims=True))
                - jnp.exp(jnp.sum(lam[2:3] * lam[3:4], axis=-1, keepdims=True)) + lam_init)
    rows = pl.ds(pl.multiple_of(qi * tq, tq), tq)
    q = _rope(q_ref[...], cos_ref[rows, :], sin_ref[rows, :]) * ((HEAD_W // 2) ** -0.5 * LOG2E)
    lane = lax.broadcasted_iota(jnp.int32, q.shape, 1)
    q0 = jnp.where(lane < HEAD_W // 2, q, 0.0).astype(BF16)
    q1 = jnp.where(lane >= HEAD_W // 2, q, 0.0).astype(BF16)

    def attend(r0, r1, kb, vb):
        s0 = _dot_nt(q0[r0:r1], kb)
        s1 = _dot_nt(q1[r0:r1], kb)
        e0 = jnp.exp2(s0 - jnp.max(s0, axis=-1, keepdims=True)).astype(BF16)
        e1 = jnp.exp2(s1 - jnp.max(s1, axis=-1, keepdims=True)).astype(BF16)
        t0 = _dot(e0, vb)
        t1 = _dot(e1, vb)
        o = t0[:, :HEAD_W] / t0[:, HEAD_W:] - lam_full * (t1[:, :HEAD_W] / t1[:, HEAD_W:])
        y_ref[r0:r1, :] = (_rms(o) * nw_ref[...] * (1.0 - lam_init)).astype(y_ref.dtype)

    @pl.when(qi == 0)
    def _():
        attend(0, n_ctx, kb_ref[0:n_ctx, :], vb_ref[0:n_ctx, :])
        if tq > n_ctx:
            attend(n_ctx, tq, kb_ref[...], vb_ref[...])

    @pl.when(qi > 0)
    def _():
        attend(0, tq, kb_ref[...], vb_ref[...])


def _diff(p, cos, sin_signed, lam, norm_w, l, n_ctx, lam_init, cast_src=None):
    b, t, _ = p.shape
    tq = DIFF_Q_TILE if (t % DIFF_Q_TILE == 0 and DIFF_Q_TILE % n_ctx == 0) else n_ctx
    base = 5 * HEADS

    def col(seg, rows, rmap):
        return pl.BlockSpec((None, rows, HEAD_W), lambda bi, h, qi: (bi, rmap(qi), base + seg * HEADS + h))

    return _mixer_call(
        functools.partial(_diff_kernel, n_ctx=n_ctx, tq=tq, lam_init=lam_init),
        (b, HEADS, t // tq),
        [col(0, tq, lambda qi: qi), col(1, t, lambda qi: 0), col(2, t, lambda qi: 0),
         pl.BlockSpec((t, HEAD_W), lambda bi, h, qi: (0, 0)),
         pl.BlockSpec((t, HEAD_W), lambda bi, h, qi: (0, 0)),
         pl.BlockSpec((None, 8, HEAD_W), lambda bi, h, qi: (l, 0, 0)),
         pl.BlockSpec((None, 1, HEAD_W), lambda bi, h, qi: (l, 0, h))],
        pl.BlockSpec((None, tq, HEAD_W), lambda bi, h, qi: (bi, qi, h)),
        jax.ShapeDtypeStruct((b, t, GROUP_W), BF16),
        [pltpu.VMEM((t, HEAD_W), BF16), pltpu.VMEM((t, 2 * HEAD_W), BF16)],
        (p, p, p, cos, sin_signed, lam, norm_w), cast_src, l + 1)


def _na_kernel(q_ref, k_ref, v_ref, bias_ref, y_ref, kb_ref, vb_ref, *, n_ctx, grid_rows, kh):
    scale = HEAD_W ** -0.5
    kb_ref[...] = k_ref[...].astype(BF16)
    vb_ref[...] = v_ref[...].astype(BF16)
    kc = kb_ref[0:n_ctx, :]
    vc = vb_ref[0:n_ctx, :]

    qc = (q_ref[0:n_ctx, :] * scale).astype(BF16)
    pc = _softmax_rows(_dot_nt(qc, kc))
    y_ref[0:n_ctx, :] = _dot(pc.astype(BF16), vc).astype(y_ref.dtype)

    win = kh * GRID_W

    def one_row(r, out):
        rows = pl.ds(pl.multiple_of(n_ctx + r * GRID_W, GRID_W), GRID_W)
        q = (q_ref[rows, :] * scale).astype(BF16)
        start = jnp.clip(r - kh // 2, 0, grid_rows - kh)
        wrows = pl.ds(pl.multiple_of(n_ctx + start * GRID_W, GRID_W), win)
        sb = _dot_nt(q, kb_ref[wrows, :]) + bias_ref[r - start]
        sc = _dot_nt(q, kc)
        yield
        m = jnp.maximum(jnp.max(sb, axis=-1, keepdims=True), jnp.max(sc, axis=-1, keepdims=True))
        eb = jnp.exp(sb - m)
        ec = jnp.exp(sc - m)
        inv = 1.0 / (jnp.sum(eb, axis=-1, keepdims=True) + jnp.sum(ec, axis=-1, keepdims=True))
        pb = (eb * inv).astype(BF16)
        pc_ = (ec * inv).astype(BF16)
        yield
        out.append(_dot(pb, vb_ref[wrows, :]) + _dot(pc_, vc))

    nu = _unroll(grid_rows, NA_UNROLL)

    def body(it, carry):
        outs = [[] for _ in range(nu)]
        _round_robin([one_row(it * nu + u, outs[u]) for u in range(nu)])
        o = jnp.concatenate([x[0] for x in outs], axis=0)
        rows = pl.ds(pl.multiple_of(n_ctx + it * (nu * GRID_W), GRID_W), nu * GRID_W)
        y_ref[rows, :] = o.astype(y_ref.dtype)
        return carry

    lax.fori_loop(0, grid_rows // nu, body, 0)


def _na(p, bias, l, n_ctx, cast_src=None):
    b, t, _ = p.shape
    grid_rows = (t - n_ctx) // GRID_W
    kh = min(NA_KH, grid_rows)
    base = 8 * HEADS

    def col(seg):
        return pl.BlockSpec((None, t, HEAD_W), lambda bi, h: (bi, 0, base + seg * HEADS + h))

    return _mixer_call(
        functools.partial(_na_kernel, n_ctx=n_ctx, grid_rows=grid_rows, kh=kh),
        (b, HEADS),
        [col(0), col(1), col(2),
         pl.BlockSpec((None, None, kh, GRID_W, kh * GRID_W), lambda bi, h: (l, h, 0, 0, 0))],
        pl.BlockSpec((None, t, HEAD_W), lambda bi, h: (bi, 0, h)),
        jax.ShapeDtypeStruct((b, t, GROUP_W), BF16),
        [pltpu.VMEM((t, HEAD_W), BF16), pltpu.VMEM((t, HEAD_W), BF16)],
        (p, p, p, bias), cast_src, l + 1)


def _mlstm_chunk(q_ref, k_ref, v_ref, gr_ref, fb, state, h_ref, r0, rev):
    L = MLSTM_CHUNK
    rows = pl.ds(r0, L)
    gi, gf = (2, 3) if rev else (0, 1)
    q = (q_ref[rows, :] * (HEAD_W ** -0.5)).astype(BF16)
    k = k_ref[rows, :]
    v = v_ref[rows, :]
    vaug = jnp.concatenate([v, jnp.ones_like(v)], axis=1).astype(BF16)
    grow = gr_ref[:, rows]
    i_row = grow[gi:gi + 1, :]
    f_rows = _log_sigmoid(grow + fb)

    t_i = lax.broadcasted_iota(jnp.int32, (L, L), 0)
    s_i = lax.broadcasted_iota(jnp.int32, (L, L), 1)
    causal = (s_i <= t_i) if not rev else (s_i >= t_i)
    b_col = jnp.sum(jnp.where(causal, f_rows[gf:gf + 1, :], 0.0), axis=1, keepdims=True)
    tri_t = jnp.where((t_i <= s_i) if not rev else (t_i >= s_i), 1.0, 0.0).astype(BF16)
    fr3 = _split3(f_rows)
    b_row = (_dot(fr3[0], tri_t) + _dot(fr3[1], tri_t) + _dot(fr3[2], tri_t))[gf:gf + 1, :]
    b_tot = b_row[:, L - 1:L] if not rev else b_row[:, 0:1]
    log_w = jnp.where(causal, b_col - b_row + i_row, NEG)
    row_max = jnp.max(log_w, axis=1, keepdims=True)
    upd_row = b_tot - b_row + i_row
    upd_max = jnp.max(upd_row, axis=1, keepdims=True)
    s_qk = _dot_nt(q, k.astype(BF16))
    k_t = k.T
    yield

    m = state[1]
    inter = b_col + m
    m_t = jnp.maximum(inter, row_max)
    w_inter = jnp.exp(inter - m_t)
    w_intra = (jnp.exp(log_w - m_t) * s_qk).astype(BF16)
    m_new = jnp.maximum(b_tot + m, upd_max)
    w_s = jnp.exp(b_tot + m - m_new)
    state[1] = m_new
    upd = _dot((k_t * jnp.exp(upd_row - m_new)).astype(BF16), vaug)
    intra = _dot(w_intra, vaug)
    yield

    c_aug = state[0]
    tot = w_inter * _dot(q, c_aug.astype(BF16)) + intra
    state[0] = w_s * c_aug + upd
    h_ref[rows, :] = tot[:, :HEAD_W] / jnp.maximum(jnp.abs(tot[:, HEAD_W:]), jnp.exp(-m_t))


def _mlstm_kernel(fb_ref, q_ref, k_ref, v_ref, og_ref, gr_ref, nw_ref, y_ref,
                  hf_ref, hb_ref, cf_ref, cb_ref, mf_ref, mb_ref, *, n_ctx, n_tok):
    L = MLSTM_CHUNK
    nc, ncc = n_tok // L, n_ctx // L
    U = _unroll(nc, MLSTM_UNROLL)
    h = pl.program_id(1)
    fb_f = fb_ref[0, h]
    fb_b = fb_ref[1, h]
    for ref in (cf_ref, cb_ref, mf_ref, mb_ref):
        ref[...] = jnp.zeros_like(ref)

    def body(it, carry):
        sf = [cf_ref[...], mf_ref[...]]
        sb = [cb_ref[...], mb_ref[...]]
        gens = []
        for u in range(U):
            step = it * U + u
            gens.append(_mlstm_chunk(q_ref, k_ref, v_ref, gr_ref, fb_f, sf, hf_ref,
                                     pl.multiple_of(step * L, L), False))
            ck = jnp.where(step < ncc, ncc - 1 - step, nc - 1 - (step - ncc))
            gens.append(_mlstm_chunk(q_ref, k_ref, v_ref, gr_ref, fb_b, sb, hb_ref,
                                     pl.multiple_of(ck * L, L), True))
        _round_robin(gens)
        cf_ref[...], mf_ref[...] = sf
        cb_ref[...], mb_ref[...] = sb
        return carry

    lax.fori_loop(0, nc // U, body, 0)

    fr = 256
    def fin(i, carry):
        rows = pl.ds(pl.multiple_of(i * fr, fr), fr)
        o = jax.nn.sigmoid(og_ref[rows, :]) * (hf_ref[rows, :] + hb_ref[rows, :])
        y_ref[rows, :] = (_rms(o) * nw_ref[...]).astype(y_ref.dtype)
        return carry

    lax.fori_loop(0, n_tok // fr, fin, 0)


def _mlstm(p, grow, f_bias, norm_w, l, n_ctx, cast_src=None):
    b, t, _ = p.shape
    base = 11 * HEADS

    def col(seg):
        return pl.BlockSpec((None, t, HEAD_W), lambda bi, h: (bi, 0, base + seg * HEADS + h))

    return _mixer_call(
        functools.partial(_mlstm_kernel, n_ctx=n_ctx, n_tok=t),
        (b, HEADS),
        [pl.BlockSpec(memory_space=pltpu.SMEM),
         col(0), col(1), col(2), col(3),
         pl.BlockSpec((None, None, 8, t), lambda bi, h: (bi, h, 0, 0)),
         pl.BlockSpec((None, 1, HEAD_W), lambda bi, h: (l, 0, h))],
        pl.BlockSpec((None, t, HEAD_W), lambda bi, h: (bi, 0, h)),
        jax.ShapeDtypeStruct((b, t, GROUP_W), BF16),
        [pltpu.VMEM((t, HEAD_W), F32), pltpu.VMEM((t, HEAD_W), F32),
         pltpu.VMEM((HEAD_W, 2 * HEAD_W), F32), pltpu.VMEM((HEAD_W, 2 * HEAD_W), F32),
         pltpu.VMEM((1, 1), F32), pltpu.VMEM((1, 1), F32)],
        (f_bias, p, p, p, p, grow, norm_w), cast_src, l + 1)


def _outproj_kernel(ya_ref, yb_ref, yc_ref, yd_ref, w_ref, x_ref, mx_ref, mc_ref, o_ref, *, n_ctx, tm):
    i = pl.program_id(1)
    acc = _dot(ya_ref[...], w_ref[0])
    acc += _dot(yb_ref[...], w_ref[1])
    acc += _dot(yc_ref[...], w_ref[2])
    acc += _dot(yd_ref[...], w_ref[3])
    rows = i * tm + lax.broadcasted_iota(jnp.int32, (tm, 1), 0)
    gate = jnp.where(rows < n_ctx, mc_ref[2:3, :], mx_ref[2:3, :])
    o_ref[...] = x_ref[...] + gate * acc


def _outproj(ys, w_out, xs, mods, l, n_ctx):
    b, t, d = xs.shape
    tm, tn = ROW_TILE, 1024
    nb = mods.shape[1] - 1
    yspec = pl.BlockSpec((None, tm, GROUP_W), lambda bi, i, j: (bi, i, 0))
    return pl.pallas_call(
        functools.partial(_outproj_kernel, n_ctx=n_ctx, tm=tm),
        grid=(b, t // tm, d // tn),
        in_specs=[yspec, yspec, yspec, yspec,
                  pl.BlockSpec((4, GROUP_W, tn), lambda bi, i, j: (0, 0, j)),
                  pl.BlockSpec((None, tm, tn), lambda bi, i, j: (bi, i, j)),
                  pl.BlockSpec((None, None, 6, tn), lambda bi, i, j: (l, bi, 0, j)),
                  pl.BlockSpec((None, None, 6, tn), lambda bi, i, j: (l, nb, 0, j))],
        out_specs=pl.BlockSpec((None, tm, tn), lambda bi, i, j: (bi, i, j)),
        out_shape=jax.ShapeDtypeStruct((b, t, d), F32),
        compiler_params=_cparams("arbitrary", "arbitrary", "arbitrary"),
    )(*ys, w_out, xs, mods, mods)


def _ffn_kernel(xm_ref, xp_ref, xn_ref, mx_ref, mc_ref, wa_ref, wg_ref, cwa_ref, cwg_ref,
                cba_ref, cbg_ref, wd_ref, o_ref, hs_ref, ua_ref, ug_ref, *, n_ctx, n_tok, tm):
    assert n_ctx % FFN_SUB == 0 and n_tok % FFN_SUB == 0 and tm % FFN_SUB == 0
    i = pl.program_id(1)
    f = pl.program_id(2)
    nf = pl.num_programs(2) - 1
    row0 = i * tm
    cur = f % 2
    prev = 1 - cur

    def up_project():
        ua_ref[cur] = _dot(hs_ref[...], wa_ref[...])
        ug_ref[cur] = _dot(hs_ref[...], wg_ref[...])

    def finish():
        ua = ua_ref.at[prev]
        ug = ug_ref.at[prev]
        for r in range(0, tm, FFN_SUB):
            g0 = row0 + r
            no_prev = (g0 == 0) | (g0 == n_ctx)
            no_next = (g0 + FFN_SUB == n_ctx) | (g0 + FFN_SUB == n_tok)
            tile_row = lax.broadcasted_iota(jnp.int32, (8, 1), 0)
            drop_first = (tile_row == 0) & no_prev
            drop_last = (tile_row == 7) & no_next

            def conv(u_ref, cw_ref, cb_ref):
                lo = HALO + r
                before = u_ref[lo - 1:lo - 1 + FFN_SUB, :]
                after = u_ref[lo + 1:lo + 1 + FFN_SUB, :]
                before = jnp.concatenate([jnp.where(drop_first, 0.0, before[:8]), before[8:]], axis=0)
                after = jnp.concatenate([after[:-8], jnp.where(drop_last, 0.0, after[-8:])], axis=0)
                return (before * cw_ref[0:1, :] + u_ref[lo:lo + FFN_SUB, :] * cw_ref[1:2, :]
                        + after * cw_ref[2:3, :] + cb_ref[...])

            act = (_silu(conv(ua, cwa_ref, cba_ref)) * conv(ug, cwg_ref, cbg_ref)).astype(BF16)
            o_ref[r:r + FFN_SUB, :] += _dot(act, wd_ref[...])

    @pl.when(f == 0)
    def _():
        hs_ref[0:HALO, :] = _modulate(xp_ref[...], mx_ref, mc_ref, row0 - HALO, 3, n_ctx).astype(BF16)
        _modulate_into(hs_ref, HALO, xm_ref, mx_ref, mc_ref, row0, 3, n_ctx)
        hs_ref[HALO + tm:, :] = _modulate(xn_ref[...], mx_ref, mc_ref, row0 + tm, 3, n_ctx).astype(BF16)
        o_ref[...] = jnp.zeros_like(o_ref)
        up_project()

    @pl.when((f > 0) & (f < nf))
    def _():
        up_project()
        finish()

    @pl.when(f == nf)
    def _():
        finish()
        rows = row0 + lax.broadcasted_iota(jnp.int32, (tm, 1), 0)
        gate = jnp.where(rows < n_ctx, mc_ref[5:6, :], mx_ref[5:6, :])
        o_ref[...] = xm_ref[...] + gate * o_ref[...]


def _ffn(xs, mods, w_up, conv_w, conv_b, w_down, l, n_ctx):
    b, t, d = xs.shape
    tm, tf = FFN_ROW_TILE, FF_TILE
    d_ff = w_down.shape[0]
    nf = d_ff // tf
    nb = mods.shape[1] - 1
    hb = tm // HALO
    last = t // HALO - 1

    def up(f):
        return jnp.minimum(f, nf - 1)

    def fin(f):
        return jnp.maximum(f - 1, 0)

    return pl.pallas_call(
        functools.partial(_ffn_kernel, n_ctx=n_ctx, n_tok=t, tm=tm),
        grid=(b, t // tm, nf + 1),
        in_specs=[
            pl.BlockSpec((None, tm, d), lambda bi, i, f: (bi, i, 0)),
            pl.BlockSpec((None, HALO, d), lambda bi, i, f: (bi, jnp.maximum(i * hb - 1, 0), 0)),
            pl.BlockSpec((None, HALO, d), lambda bi, i, f: (bi, jnp.minimum((i + 1) * hb, last), 0)),
            pl.BlockSpec((None, None, 6, d), lambda bi, i, f: (l, bi, 0, 0)),
            pl.BlockSpec((None, None, 6, d), lambda bi, i, f: (l, nb, 0, 0)),
            pl.BlockSpec((d, tf), lambda bi, i, f: (0, up(f))),
            pl.BlockSpec((d, tf), lambda bi, i, f: (0, nf + up(f))),
            pl.BlockSpec((None, CONV_W, tf), lambda bi, i, f: (l, 0, fin(f))),
            pl.BlockSpec((None, CONV_W, tf), lambda bi, i, f: (l, 0, nf + fin(f))),
            pl.BlockSpec((None, 1, tf), lambda bi, i, f: (l, 0, fin(f))),
            pl.BlockSpec((None, 1, tf), lambda bi, i, f: (l, 0, nf + fin(f))),
            pl.BlockSpec((tf, d), lambda bi, i, f: (fin(f), 0)),
        ],
        out_specs=pl.BlockSpec((None, tm, d), lambda bi, i, f: (bi, i, 0)),
        out_shape=jax.ShapeDtypeStruct((b, t, d), F32),
        scratch_shapes=[pltpu.VMEM((tm + 2 * HALO, d), BF16),
                        pltpu.VMEM((2, tm + 2 * HALO, tf), F32),
                        pltpu.VMEM((2, tm + 2 * HALO, tf), F32)],
        compiler_params=_cparams("arbitrary", "arbitrary", "arbitrary"),
    )(xs, xs, xs, mods, mods, w_up, w_up, conv_w, conv_w, conv_b, conv_b, w_down)


def _final_kernel(x_ref, w_ref, o_ref):
    o_ref[...] = _rms(x_ref[...]) * w_ref[...]


def _final_norm(xs, w, n_ctx):
    b, t, d = xs.shape
    tr = n_ctx
    off = n_ctx // tr
    return pl.pallas_call(
        _final_kernel,
        grid=(b, (t - n_ctx) // tr),
        in_specs=[pl.BlockSpec((None, tr, d), lambda bi, i: (bi, i + off, 0)),
                  pl.BlockSpec((1, d), lambda bi, i: (0, 0))],
        out_specs=pl.BlockSpec((None, tr, d), lambda bi, i: (bi, i, 0)),
        out_shape=jax.ShapeDtypeStruct((b, t - n_ctx, d), F32),
        compiler_params=_cparams("arbitrary", "arbitrary"),
    )(xs, w.reshape(1, d))


def _lambda_init(layer_idx):
    return 0.8 - 0.6 * math.exp(-0.3 * layer_idx)


def _rope_tables(n_ctx, n_lat):
    dh = HEAD_W // 2
    t = np.arange(n_lat)
    row = (t // GRID_W).astype(np.float32)
    colp = (t % GRID_W).astype(np.float32)
    half = dh // 2
    inv = (ROPE_THETA ** (-np.arange(0, half, 2, dtype=np.float32) / half)).astype(np.float32)
    ar = row[:, None] * inv
    ac = colp[:, None] * inv
    ang = jnp.asarray(np.concatenate([ar, ar, ac, ac], axis=-1))
    cos, sin = jnp.cos(ang), jnp.sin(ang)
    sign = np.where((np.arange(dh) % 32) < 16, -1.0, 1.0).astype(np.float32)
    cos = jnp.concatenate([jnp.ones((n_ctx, dh), F32), cos], axis=0)
    sin = jnp.concatenate([jnp.zeros((n_ctx, dh), F32), sin * sign], axis=0)
    return jnp.tile(cos, (1, 2)), jnp.tile(sin, (1, 2))


def _na_bias(na_rpb, kh):
    w = np.arange(GRID_W)
    col_start = np.clip(w - NA_KW // 2, 0, GRID_W - NA_KW)
    col_ok = (w[None, :] >= col_start[:, None]) & (w[None, :] < col_start[:, None] + NA_KW)
    coff = np.clip(w[None, :] - w[:, None], -(NA_KW - 1), NA_KW - 1) + (NA_KW - 1)
    onehot = jnp.asarray(coff[:, :, None] == np.arange(2 * NA_KW - 1), F32)
    toe = jnp.einsum('dhrc,qkc->dhrqk', na_rpb.astype(F32), onehot, precision=lax.Precision.HIGHEST)
    toe = jnp.where(col_ok, toe, NEG).transpose(0, 1, 3, 2, 4)
    shape = (na_rpb.shape[0], na_rpb.shape[1], GRID_W, kh * GRID_W)
    per_id = [toe[:, :, :, NA_KH - 1 - i:NA_KH - 1 - i + kh].reshape(shape) for i in range(kh)]
    return jnp.stack(per_id, axis=2)


def kernel(x, c, ctx, c_ctx, w_ada, b_ada, w_in, hgrn_lb, hgrn_norm_w, diff_lam, diff_norm_w, na_rpb,
           mlstm_f_bias, mlstm_norm_w, w_out, w_up, conv_w, conv_b, w_down, final_norm_w):
    bsz, n_lat, d = x.shape
    n_ctx = ctx.shape[1]
    depth = w_in.shape[0]
    d_ff = w_down.shape[1]
    grid_rows = n_lat // GRID_W

    xs = jnp.concatenate([ctx, x], axis=1)
    n_mod = -(-(bsz + 1) // 8) * 8
    cc = jnp.zeros((n_mod, d), F32).at[:bsz].set(c).at[n_mod - 1].set(c_ctx)
    mods = _ada(cc, w_ada, b_ada).reshape(depth, n_mod, 6, d)

    w_in_all = w_in.astype(BF16)
    w_out_b, w_up_b, w_down_b = (w[0].astype(BF16) for w in (w_out, w_up, w_down))
    w_gate_b = jnp.pad(w_in[:, :, D_MAIN:], ((0, 0), (0, 0), (0, HEAD_W - N_GATES))).astype(BF16)
    conv_b3 = conv_b.reshape(depth, 1, 2 * d_ff)

    lb = jnp.cumsum(jax.nn.softmax(hgrn_lb.astype(F32), axis=0), axis=0)
    lb = lb - lb[:1]
    zero = jnp.zeros_like(lb[:, :1])
    gate_par = jnp.concatenate([jnp.log(lb[:, 0:1]), jnp.log1p(-lb[:, 0:1]), 1.0 - lb[:, 0:1],
                                jnp.log(lb[:, 1:2]), jnp.log1p(-lb[:, 1:2]), 1.0 - lb[:, 1:2],
                                zero, zero], axis=1)
    cos, sin_signed = _rope_tables(n_ctx, n_lat)
    lam = jnp.pad(diff_lam.astype(F32), ((0, 0), (0, 4), (0, HEAD_W - diff_lam.shape[2])))
    bias = _na_bias(na_rpb, min(NA_KH, grid_rows))
    f_bias = mlstm_f_bias.astype(F32)

    def nw(w):
        return w.astype(F32).reshape(depth, 1, GROUP_W)

    hgrn_nw, diff_nw, mlstm_nw = nw(hgrn_norm_w), nw(diff_norm_w), nw(mlstm_norm_w)

    for l in range(depth):
        p, gates = _inproj(xs, mods, w_in_all, w_gate_b, l, n_ctx)
        g4 = gates[:, :, :N_GATES].reshape(bsz, -1, 4, HEADS)
        grow = jnp.pad(g4.transpose(0, 3, 2, 1), ((0, 0), (0, 0), (0, 4), (0, 0)))
        nxt = l + 1 < depth
        ya, w_up_n = _hgrn(p, gate_par, hgrn_nw, l, n_ctx, w_up if nxt else None)
        yb, _ = _diff(p, cos, sin_signed, lam, diff_nw, l, n_ctx, _lambda_init(l))
        yc, w_down_n = _na(p, bias, l, n_ctx, w_down if nxt else None)
        yd, w_out_n = _mlstm(p, grow, f_bias[l], mlstm_nw, l, n_ctx, w_out if nxt else None)
        xs = _outproj((ya, yb, yc, yd), w_out_b.reshape(4, GROUP_W, d), xs, mods, l, n_ctx)
        xs = _ffn(xs, mods, w_up_b, conv_w, conv_b3, w_down_b, l, n_ctx)
        w_out_b, w_up_b, w_down_b = w_out_n, w_up_n, w_down_n
    return _final_norm(xs, final_norm_w, n_ctx)
```
